```python
import jax, jax.numpy as jnp
from jax import lax
import numpy as np

D_MODEL = 2048
BATCH = 8
SEQ = 2048
DEPTH = 2

BRANCH_WIDTH = D_MODEL // 2
N_BRANCH = 4
SSD_HEAD_DIM = 64
SSD_HEADS = BRANCH_WIDTH // SSD_HEAD_DIM
SSD_GROUPS = 4
SSD_STATE = 128
SSD_CONV = 4
SSD_CHUNK = 128
SSD_XBC = BRANCH_WIDTH + 2 * SSD_GROUPS * SSD_STATE
ATTN_HEAD_DIM = 64
ATTN_Q_HEADS = BRANCH_WIDTH // ATTN_HEAD_DIM
ATTN_KV_HEADS = 4
KV_WIDTH = ATTN_KV_HEADS * ATTN_HEAD_DIM
WINDOW = 128
ROPE_THETA = 10000.0
SCONV_WIDTH = 3
LRU_WIDTH = BRANCH_WIDTH
LRU_BLOCKS = 16
LRU_BLOCK_DIM = LRU_WIDTH // LRU_BLOCKS
LRU_CONV = 4
LRU_C = 8.0
LN_EPS = 1e-5
RMS_EPS = 1e-5
DEEPNORM_ALPHA = (2.0 * DEPTH) ** 0.25
DEEPNORM_BETA = (8.0 * DEPTH) ** -0.25
ADA_INIT = 0.5

COL_SIZES = (
    BRANCH_WIDTH, SSD_XBC, SSD_HEADS,
    BRANCH_WIDTH, KV_WIDTH, KV_WIDTH, BRANCH_WIDTH,
    BRANCH_WIDTH, BRANCH_WIDTH, BRANCH_WIDTH, BRANCH_WIDTH,
    LRU_WIDTH, LRU_WIDTH,
    N_BRANCH * D_MODEL,
)
IN_COLS = sum(COL_SIZES)
SPLIT_POINTS = tuple(sum(COL_SIZES[:i + 1]) for i in range(len(COL_SIZES) - 1))

kernel_name = "hybrid_gated_parallel_mixers_deepnorm_adaln"


def layer_norm(x, w, b):
    xf = x.astype(jnp.float32)
    mu = xf.mean(-1, keepdims=True)
    var = jnp.square(xf - mu).mean(-1, keepdims=True)
    y = (xf - mu) * lax.rsqrt(var + LN_EPS) * w.astype(jnp.float32) + b.astype(jnp.float32)
    return y.astype(x.dtype)


def rms_norm(x):
    xf = x.astype(jnp.float32)
    return xf * lax.rsqrt(jnp.square(xf).mean(-1, keepdims=True) + RMS_EPS)


def causal_dwconv(x, w, b=None):
    k = w.shape[0]
    y = lax.conv_general_dilated(
        x, w[:, None, :].astype(x.dtype), window_strides=(1,), padding=[(k - 1, 0)],
        dimension_numbers=('NWC', 'WIO', 'NWC'), feature_group_count=x.shape[-1])
    return y if b is None else y + b.astype(x.dtype)


def rope(x, pos):
    half = x.shape[-1] // 2
    inv_freq = ROPE_THETA ** (-jnp.arange(half, dtype=jnp.float32) / half)
    ang = pos.astype(jnp.float32)[..., None] * inv_freq
    cos, sin = jnp.cos(ang)[:, :, None, :], jnp.sin(ang)[:, :, None, :]
    xf = x.astype(jnp.float32)
    x1, x2 = xf[..., :half], xf[..., half:]
    return jnp.concatenate([x1 * cos - x2 * sin, x2 * cos + x1 * sin], -1).astype(x.dtype)


def ssd_branch(z, xbc, dt, conv_w, conv_b, dt_bias, a_log, d_skip, norm_w):
    f32 = jnp.float32
    b, l, _ = z.shape
    nc, q, g, r = l // SSD_CHUNK, SSD_CHUNK, SSD_GROUPS, SSD_HEADS // SSD_GROUPS
    xbc = jax.nn.silu(causal_dwconv(xbc, conv_w, conv_b))
    xs, bm, cm = jnp.split(xbc, [BRANCH_WIDTH, BRANCH_WIDTH + SSD_GROUPS * SSD_STATE], axis=-1)
    xs = xs.astype(f32).reshape(b, l, SSD_HEADS, SSD_HEAD_DIM)
    dt = jax.nn.softplus(dt.astype(f32) + dt_bias.astype(f32))
    a = -jnp.exp(a_log.astype(f32))
    xdt = (xs * dt[..., None]).reshape(b, nc, q, g, r, SSD_HEAD_DIM)
    da = (dt * a).reshape(b, nc, q, g, r)
    bm = bm.astype(f32).reshape(b, nc, q, g, SSD_STATE)
    cm = cm.astype(f32).reshape(b, nc, q, g, SSD_STATE)
    cum = jnp.cumsum(da, axis=2)
    causal = jnp.tril(jnp.ones((q, q), bool))[None, None, :, :, None, None]
    seg = cum[:, :, :, None] - cum[:, :, None, :]
    decay = jnp.exp(jnp.where(causal, seg, -jnp.inf))
    cb = jnp.einsum('bclgn,bcsgn->bclsg', cm, bm)
    y_diag = jnp.einsum('bclsg,bclsgr,bcsgrp->bclgrp', cb, decay, xdt)
    decay_to_end = jnp.exp(cum[:, :, -1:] - cum)
    states = jnp.einsum('bcsgn,bcsgr,bcsgrp->bcgrpn', bm, decay_to_end, xdt)
    chunk_decay = jnp.exp(cum[:, :, -1])

    def step(h, inp):
        s, dcy = inp
        return h * dcy[..., None, None] + s, h

    h0 = jnp.zeros((b, g, r, SSD_HEAD_DIM, SSD_STATE), f32)
    _, prev = lax.scan(step, h0, (jnp.moveaxis(states, 1, 0), jnp.moveaxis(chunk_decay, 1, 0)))
    prev = jnp.moveaxis(prev, 0, 1)
    y_off = jnp.einsum('bclgn,bcgrpn,bclgr->bclgrp', cm, prev, jnp.exp(cum))
    y = (y_diag + y_off).reshape(b, l, SSD_HEADS, SSD_HEAD_DIM) + xs * d_skip.astype(f32)[:, None]
    y = y.reshape(b, l, BRANCH_WIDTH) * jax.nn.silu(z.astype(f32))
    return (rms_norm(y) * norm_w.astype(f32)).astype(z.dtype)


def swa_branch(q, k, v, gate, pos, sinks):
    f32 = jnp.float32
    b, l, _ = q.shape
    nb, r = l // WINDOW, ATTN_Q_HEADS // ATTN_KV_HEADS
    q = rope(q.reshape(b, l, ATTN_Q_HEADS, ATTN_HEAD_DIM), pos)
    k = rope(k.reshape(b, l, ATTN_KV_HEADS, ATTN_HEAD_DIM), pos)
    v = v.reshape(b, l, ATTN_KV_HEADS, ATTN_HEAD_DIM)
    qb = q.reshape(b, nb, WINDOW, ATTN_KV_HEADS, r, ATTN_HEAD_DIM)

    def with_prev(t):
        tb = t.reshape(b, nb, WINDOW, ATTN_KV_HEADS, ATTN_HEAD_DIM)
        prev = jnp.pad(tb, ((0, 0), (1, 0), (0, 0), (0, 0), (0, 0)))[:, :-1]
        return jnp.concatenate([prev, tb], axis=2)

    kb, vb = with_prev(k), with_prev(v)
    s = jnp.einsum('bnqkrd,bnskd->bnkrqs', qb, kb).astype(f32) * (ATTN_HEAD_DIM ** -0.5)
    qi = jnp.arange(WINDOW)[:, None]
    sj = jnp.arange(2 * WINDOW)[None, :]
    rel = qi + WINDOW - sj
    band = (rel >= 0) & (rel < WINDOW)
    blk = jnp.arange(nb)[:, None, None]
    valid = band[None] & ((blk > 0) | (sj[None] >= WINDOW))
    s = jnp.where(valid[None, :, None, None], s, -jnp.inf)
    sink = sinks.astype(f32).reshape(ATTN_KV_HEADS, r)[None, None, :, :, None, None]
    m = jnp.maximum(s.max(-1, keepdims=True), sink)
    e = jnp.exp(s - m)
    p = e / (e.sum(-1, keepdims=True) + jnp.exp(sink - m))
    o = jnp.einsum('bnkrqs,bnskd->bnqkrd', p.astype(vb.dtype), vb).reshape(b, l, BRANCH_WIDTH)
    return o * jax.nn.silu(gate)


def shortconv_branch(bg, cg, xs, gate, conv_w):
    return bg * causal_dwconv(cg * xs, conv_w) * jax.nn.silu(gate)


def rglru_branch(xs, gate, conv_w, conv_b, w_a, b_a, w_x, b_x, lam):
    f32 = jnp.float32
    b, l, _ = xs.shape
    xs = causal_dwconv(xs, conv_w, conv_b)
    xh = xs.reshape(b, l, LRU_BLOCKS, LRU_BLOCK_DIM)
    rg = jax.nn.sigmoid(jnp.einsum('blhi,hij->blhj', xh, w_a).reshape(b, l, LRU_WIDTH) + b_a)
    ig = jax.nn.sigmoid(jnp.einsum('blhi,hij->blhj', xh, w_x).reshape(b, l, LRU_WIDTH) + b_x)
    log_a = -LRU_C * rg.astype(f32) * jax.nn.softplus(-lam.astype(f32))
    a = jnp.exp(log_a)
    u = jnp.sqrt(-jnp.expm1(2.0 * log_a)) * (ig * xs).astype(f32)

    def combine(left, right):
        a1, b1 = left
        a2, b2 = right
        return a1 * a2, a2 * b1 + b2

    _, h = lax.associative_scan(combine, (a, u), axis=1)
    return (h * jax.nn.silu(gate.astype(f32))).astype(xs.dtype)


def hybrid_layer(x, pos, c_act, w_ada, b_ada, w_in, b_gate, ssd_conv_w, ssd_conv_b, ssd_dt_bias,
                 ssd_a_log, ssd_d, ssd_norm_w, attn_sinks, sconv_w, lru_conv_w, lru_conv_b,
                 lru_w_a, lru_b_a, lru_w_x, lru_b_x, lru_lambda, w_branch, w_out, ln_w, ln_b):
    b, l, d = x.shape
    ada = c_act @ w_ada + b_ada
    shift, scale, gate = jnp.split(ada, 3, axis=-1)
    h = x * (1 + scale[:, None]) + shift[:, None]
    proj = h @ w_in
    (a_z, a_xbc, a_dt, b_q, b_k, b_v, b_g, c_b, c_c, c_x, c_g, d_x, d_g, merge) = jnp.split(
        proj, SPLIT_POINTS, axis=-1)
    ys = (
        ssd_branch(a_z, a_xbc, a_dt, ssd_conv_w, ssd_conv_b, ssd_dt_bias, ssd_a_log, ssd_d, ssd_norm_w),
        swa_branch(b_q, b_k, b_v, b_g, pos, attn_sinks),
        shortconv_branch(c_b, c_c, c_x, c_g, sconv_w),
        rglru_branch(d_x, d_g, lru_conv_w, lru_conv_b, lru_w_a, lru_b_a, lru_w_x, lru_b_x, lru_lambda),
    )
    gates = jax.nn.sigmoid(merge.reshape(b, l, N_BRANCH, d) + b_gate)
    m = gates[:, :, 0] * (ys[0] @ w_branch[0])
    for i in range(1, N_BRANCH):
        m = m + gates[:, :, i] * (ys[i] @ w_branch[i])
    out = m @ w_out
    return layer_norm(DEEPNORM_ALPHA * x + gate[:, None] * out, ln_w, ln_b)


def setup_inputs(seed: int = 0) -> dict:
    key = jax.random.key(seed)
    ks = jax.random.split(key, 32)
    f32 = jnp.float32

    def nrm(k, shape, std):
        return std * jax.random.normal(k, shape, f32)

    dt0 = jnp.exp(jax.random.uniform(ks[8], (DEPTH, SSD_HEADS), f32, np.log(1e-3), np.log(1e-1)))
    a_pow = jax.random.uniform(ks[19], (DEPTH, LRU_WIDTH), f32, 0.9, 0.999)
    sig = a_pow ** (1.0 / LRU_C)
    offsets = jax.random.randint(ks[2], (BATCH, 1), 0, SEQ, jnp.int32)
    return {
        "x": nrm(ks[0], (BATCH, SEQ, D_MODEL), 1.0),
        "c": nrm(ks[1], (BATCH, D_MODEL), 1.0),
        "positions": offsets + jnp.arange(SEQ, dtype=jnp.int32)[None, :],
        "w_ada": nrm(ks[3], (DEPTH, D_MODEL, 3 * D_MODEL), ADA_INIT * D_MODEL ** -0.5),
        "b_ada": nrm(ks[4], (DEPTH, 3 * D_MODEL), 0.01),
        "w_in": nrm(ks[5], (DEPTH, D_MODEL, IN_COLS), D_MODEL ** -0.5),
        "b_gate": nrm(ks[6], (DEPTH, N_BRANCH, D_MODEL), 0.01),
        "ssd_conv_w": nrm(ks[7], (DEPTH, SSD_CONV, SSD_XBC), SSD_CONV ** -0.5),
        "ssd_conv_b": nrm(ks[9], (DEPTH, SSD_XBC), 0.01),
        "ssd_dt_bias": dt0 + jnp.log(-jnp.expm1(-dt0)),
        "ssd_a_log": jnp.log(jax.random.uniform(ks[10], (DEPTH, SSD_HEADS), f32, 1.0, 16.0)),
        "ssd_d": 1.0 + nrm(ks[11], (DEPTH, SSD_HEADS), 0.1),
        "ssd_norm_w": 1.0 + nrm(ks[12], (DEPTH, BRANCH_WIDTH), 0.1),
        "attn_sinks": nrm(ks[13], (DEPTH, ATTN_Q_HEADS), 1.0),
        "sconv_w": nrm(ks[14], (DEPTH, SCONV_WIDTH, BRANCH_WIDTH), SCONV_WIDTH ** -0.5),
        "lru_conv_w": nrm(ks[15], (DEPTH, LRU_CONV, LRU_WIDTH), LRU_CONV ** -0.5),
        "lru_conv_b": nrm(ks[16], (DEPTH, LRU_WIDTH), 0.01),
        "lru_w_a": nrm(ks[17], (DEPTH, LRU_BLOCKS, LRU_BLOCK_DIM, LRU_BLOCK_DIM), LRU_BLOCK_DIM ** -0.5),
        "lru_b_a": nrm(ks[18], (DEPTH, LRU_WIDTH), 0.01),
        "lru_w_x": nrm(ks[20], (DEPTH, LRU_BLOCKS, LRU_BLOCK_DIM, LRU_BLOCK_DIM), LRU_BLOCK_DIM ** -0.5),
        "lru_b_x": nrm(ks[21], (DEPTH, LRU_WIDTH), 0.01),
        "lru_lambda": jnp.log(sig) - jnp.log1p(-sig),
        "w_branch": nrm(ks[22], (DEPTH, N_BRANCH, BRANCH_WIDTH, D_MODEL), DEEPNORM_BETA * BRANCH_WIDTH ** -0.5),
        "w_out": nrm(ks[23], (DEPTH, D_MODEL, D_MODEL), DEEPNORM_BETA * D_MODEL ** -0.5),
        "ln_w": 1.0 + nrm(ks[24], (DEPTH, D_MODEL), 0.1),
        "ln_b": nrm(ks[25], (DEPTH, D_MODEL), 0.01),
    }


def reference(x, c, positions, w_ada, b_ada, w_in, b_gate, ssd_conv_w, ssd_conv_b, ssd_dt_bias,
              ssd_a_log, ssd_d, ssd_norm_w, attn_sinks, sconv_w, lru_conv_w, lru_conv_b, lru_w_a,
              lru_b_a, lru_w_x, lru_b_x, lru_lambda, w_branch, w_out, ln_w, ln_b):
    c_act = jax.nn.silu(c)
    for i in range(DEPTH):
        x = hybrid_layer(
            x, positions, c_act, w_ada[i], b_ada[i], w_in[i], b_gate[i], ssd_conv_w[i], ssd_conv_b[i],
            ssd_dt_bias[i], ssd_a_log[i], ssd_d[i], ssd_norm_w[i], attn_sinks[i], sconv_w[i],
            lru_conv_w[i], lru_conv_b[i], lru_w_a[i], lru_b_a[i], lru_w_x[i], lru_b_x[i], lru_lambda[i],
            w_branch[i], w_out[i], ln_w[i], ln_b[i])
    return x
```

```python
import functools

import jax
import jax.numpy as jnp
import numpy as np
from jax import lax
from jax.experimental import pallas as pl
from jax.experimental.pallas import tpu as pltpu

F32 = jnp.float32
BF16 = jnp.bfloat16

LANES = 128
SUBLANES = 8
VMEM_LIMIT = 56 * 1024 * 1024

D_MODEL = 2048
BRANCH = D_MODEL // 2
N_BRANCH = 4
HEAD_DIM = 64
SSD_HEADS = BRANCH // HEAD_DIM
SSD_GROUPS = 4
SSD_STATE = 128
SSD_CONV = 4
CHUNK = 128
ATTN_HEADS = BRANCH // HEAD_DIM
KV_HEADS = 4
KV_WIDTH = KV_HEADS * HEAD_DIM
WINDOW = 128
ROPE_THETA = 10000.0
SCONV = 3
LRU_BLOCKS = 16
LRU_CONV = 4
LRU_C = 8.0
LRU_GROUP = 256
LN_EPS = 1e-5
RMS_EPS = 1e-5
DEPTH = 2
ALPHA = (2.0 * DEPTH) ** 0.25

MERGE_COLS = N_BRANCH * D_MODEL
_C0 = MERGE_COLS // BRANCH
COL_Z, COL_XS, COL_BC, COL_Q, COL_GB = _C0, _C0 + 1, _C0 + 2, _C0 + 3, _C0 + 4
COL_CB, COL_CC, COL_CX, COL_CG, COL_DX, COL_DG = (_C0 + 5, _C0 + 6, _C0 + 7, _C0 + 8, _C0 + 9, _C0 + 10)
COL_KV = (MERGE_COLS + 11 * BRANCH) // (2 * KV_WIDTH)
ORIG_DT = 3 * BRANCH
ORIG_Q = ORIG_DT + SSD_HEADS
ORIG_K = ORIG_Q + BRANCH
ORIG_GB = ORIG_K + 2 * KV_WIDTH
ORIG_MERGE = ORIG_GB + 7 * BRANCH


def _rearrange_w_in(w_in):
    w_main = jnp.concatenate(
        [w_in[:, ORIG_MERGE:], w_in[:, :ORIG_DT], w_in[:, ORIG_Q:ORIG_K], w_in[:, ORIG_GB:ORIG_MERGE],
         w_in[:, ORIG_K:ORIG_GB]], axis=1).astype(BF16)
    w_dt = jnp.pad(w_in[:, ORIG_DT:ORIG_Q], ((0, 0), (0, LANES - SSD_HEADS))).astype(BF16)
    return w_main, w_dt


def _softplus(x):
    return jnp.maximum(x, 0.0) + jnp.log1p(jnp.exp(-jnp.abs(x)))


def _silu(x):
    return x * jax.nn.sigmoid(x)


def _cparams(sem):
    return pltpu.CompilerParams(dimension_semantics=sem, vmem_limit_bytes=VMEM_LIMIT)


def _ada_kernel(c_ref, w_ref, b_ref, o_ref):
    ca = _silu(c_ref[...])
    o_ref[...] = jnp.dot(ca, w_ref[...], preferred_element_type=F32,
                         precision=lax.Precision.HIGHEST) + b_ref[...]


def _ada(c, w_ada, b_ada):
    bsz, d = c.shape
    n = w_ada.shape[1]
    tn = 768
    return pl.pallas_call(
        _ada_kernel,
        out_shape=jax.ShapeDtypeStruct((bsz, n), F32),
        grid=(n // tn,),
        in_specs=[pl.BlockSpec((bsz, d), lambda j: (0, 0)),
                  pl.BlockSpec((d, tn), lambda j: (0, j)),
                  pl.BlockSpec((1, tn), lambda j: (0, j))],
        out_specs=pl.BlockSpec((bsz, tn), lambda j: (0, j)),
        compiler_params=_cparams(("arbitrary",)),
        name="ada",
    )(c, w_ada, b_ada.reshape(1, n))


def _inproj_kernel(x_ref, shift_ref, scale_ref, w_ref, wdt_ref, o_ref, dt_ref, h_ref):
    @pl.when(pl.program_id(1) == 0)
    def _():
        h = x_ref[...] * (1.0 + scale_ref[0]) + shift_ref[0]
        hb = h.astype(BF16)
        h_ref[...] = hb
        dt_ref[...] = jnp.dot(hb, wdt_ref[...], preferred_element_type=F32)

    o_ref[...] = jnp.dot(h_ref[...], w_ref[...], preferred_element_type=F32).astype(o_ref.dtype)


def _inproj(x2, ada3, w_main, w_dt, seq, tm, tn):
    m, d = x2.shape
    n = w_main.shape[1]
    per_b = seq // tm
    return pl.pallas_call(
        _inproj_kernel,
        out_shape=(jax.ShapeDtypeStruct((m, n), BF16), jax.ShapeDtypeStruct((m, LANES), F32)),
        grid=(m // tm, n // tn),
        in_specs=[pl.BlockSpec((tm, d), lambda i, j: (i, 0)),
                  pl.BlockSpec((1, 1, d), lambda i, j: (i // per_b, 0, 0)),
                  pl.BlockSpec((1, 1, d), lambda i, j: (i // per_b, 0, 1)),
                  pl.BlockSpec((d, tn), lambda i, j: (0, j)),
                  pl.BlockSpec((d, LANES), lambda i, j: (0, 0))],
        out_specs=(pl.BlockSpec((tm, tn), lambda i, j: (i, j)),
                   pl.BlockSpec((tm, LANES), lambda i, j: (i, 0))),
        scratch_shapes=[pltpu.VMEM((tm, d), BF16)],
        compiler_params=_cparams(("parallel", "arbitrary")),
        name="inproj",
    )(x2, ada3, ada3, w_main, w_dt)


def _pair_expand(col_a, col_b, lo):
    shape = (col_a.shape[0], LANES)
    return jnp.where(lo, jnp.broadcast_to(col_a, shape), jnp.broadcast_to(col_b, shape))


def _ssd_kernel(z_ref, xs_ref, bc_ref, dt_ref, cw_ref, cb_ref, dtb_ref, aneg_ref, dsk_ref, nw_ref,
                o_ref, cbuf_ref, state_ref, *, n_chunks):
    pad = SUBLANES

    @pl.when(pl.program_id(1) == 0)
    def _():
        cbuf_ref[0:pad, :] = jnp.zeros((pad, 2 * BRANCH), F32)
        state_ref[...] = jnp.zeros_like(state_ref)

    row = lax.broadcasted_iota(jnp.int32, (CHUNK, CHUNK), 0)
    col = lax.broadcasted_iota(jnp.int32, (CHUNK, CHUNK), 1)
    causal = row >= col
    tril = causal.astype(F32)
    lo = col < HEAD_DIM
    lo_row = lo[0:1, :]

    def chunk_body(ci, carry):
        r0 = pl.multiple_of(ci * CHUNK, CHUNK)
        cbuf_ref[pad:pad + CHUNK, 0:BRANCH] = xs_ref[pl.ds(r0, CHUNK), :].astype(F32)
        cbuf_ref[pad:pad + CHUNK, BRANCH:2 * BRANCH] = bc_ref[pl.ds(r0, CHUNK), :].astype(F32)
        conv = cb_ref[...] + cw_ref[SSD_CONV - 1:SSD_CONV, :] * cbuf_ref[pad:pad + CHUNK, :]
        for k in range(SSD_CONV - 1):
            off = pad - (SSD_CONV - 1) + k
            conv = conv + cw_ref[k:k + 1, :] * cbuf_ref[off:off + CHUNK, :]
        cbuf_ref[0:pad, :] = cbuf_ref[CHUNK:CHUNK + pad, :]
        xbc = _silu(conv)
        xc = xbc[:, 0:BRANCH]

        dt = _softplus(dt_ref[pl.ds(r0, CHUNK), :] + dtb_ref[...])
        da = dt * aneg_ref[...]
        cum = jnp.dot(tril, da, preferred_element_type=F32, precision=lax.Precision.HIGHEST)
        cum_t = cum.T

        ys = []
        for g in range(SSD_GROUPS):
            b_g = xbc[:, BRANCH + g * SSD_STATE:BRANCH + (g + 1) * SSD_STATE]
            c_g = xbc[:, BRANCH + (SSD_GROUPS + g) * SSD_STATE:BRANCH + (SSD_GROUPS + g + 1) * SSD_STATE]
            b_gb = b_g.astype(BF16)
            c_gb = c_g.astype(BF16)
            cb = lax.dot_general(c_gb, b_gb, (((1,), (1,)), ((), ())), preferred_element_type=F32)
            b_t = b_g.T.astype(BF16)
            for jp in range(2):
                j = 2 * g + jp
                h0, h1 = 2 * j, 2 * j + 1
                x_p = xc[:, j * LANES:(j + 1) * LANES]
                cum0 = jnp.broadcast_to(cum[:, h0:h0 + 1], (CHUNK, CHUNK))
                cum1 = jnp.broadcast_to(cum[:, h1:h1 + 1], (CHUNK, CHUNK))
                cum_p = jnp.where(lo, cum0, cum1)
                dt_p = _pair_expand(dt[:, h0:h0 + 1], dt[:, h1:h1 + 1], lo)
                xdt = x_p * dt_p
                xdt_lo = jnp.where(lo, xdt, 0.0).astype(BF16)
                xdt_hi = jnp.where(lo, 0.0, xdt).astype(BF16)
                l0 = jnp.exp(jnp.where(causal, cum0 - cum_t[h0:h0 + 1, :], -jnp.inf))
                l1 = jnp.exp(jnp.where(causal, cum1 - cum_t[h1:h1 + 1, :], -jnp.inf))
                y = jnp.dot((cb * l0).astype(BF16), xdt_lo, preferred_element_type=F32)
                y = y + jnp.dot((cb * l1).astype(BF16), xdt_hi, preferred_element_type=F32)
                st = state_ref[:, j * LANES:(j + 1) * LANES]
                y_off = jnp.dot(c_gb, st.astype(BF16), preferred_element_type=F32)
                y = y + y_off * jnp.exp(cum_p)
                y = y + x_p * dsk_ref[:, j * LANES:(j + 1) * LANES]
                ys.append(y)
                cum_last = cum_p[CHUNK - 1:CHUNK, :]
                xw = (xdt * jnp.exp(cum_last - cum_p)).astype(BF16)
                s_new = jnp.dot(b_t, xw, preferred_element_type=F32)
                state_ref[:, j * LANES:(j + 1) * LANES] = st * jnp.exp(cum_last) + s_new
        y = jnp.concatenate(ys, axis=1)
        y = y * _silu(z_ref[pl.ds(r0, CHUNK), :].astype(F32))
        y = y * lax.rsqrt(jnp.mean(jnp.square(y), axis=-1, keepdims=True) + RMS_EPS)
        o_ref[pl.ds(r0, CHUNK), :] = (y * nw_ref[...]).astype(o_ref.dtype)
        return carry

    lax.fori_loop(0, n_chunks, chunk_body, 0)


def _ssd(proj, dt_raw, conv_w, conv_b, dt_bias, a_log, d_skip, norm_w, bsz, seq, tt):
    m = proj.shape[0]
    per_b = seq // tt
    row_map = lambda b, t: b * per_b + t
    pad_heads = LANES - SSD_HEADS
    dtb = jnp.pad(dt_bias, (0, pad_heads)).reshape(1, LANES)
    aneg = jnp.pad(-jnp.exp(a_log), (0, pad_heads)).reshape(1, LANES)
    dsk = jnp.repeat(d_skip, HEAD_DIM).reshape(1, BRANCH)
    const = lambda shape: pl.BlockSpec(shape, lambda b, t: (0, 0))
    return pl.pallas_call(
        functools.partial(_ssd_kernel, n_chunks=tt // CHUNK),
        out_shape=jax.ShapeDtypeStruct((m, BRANCH), BF16),
        grid=(bsz, per_b),
        in_specs=[pl.BlockSpec((tt, BRANCH), lambda b, t: (row_map(b, t), COL_Z)),
                  pl.BlockSpec((tt, BRANCH), lambda b, t: (row_map(b, t), COL_XS)),
                  pl.BlockSpec((tt, BRANCH), lambda b, t: (row_map(b, t), COL_BC)),
                  pl.BlockSpec((tt, LANES), lambda b, t: (row_map(b, t), 0)),
                  const((SSD_CONV, 2 * BRANCH)), const((1, 2 * BRANCH)),
                  const((1, LANES)), const((1, LANES)), const((1, BRANCH)), const((1, BRANCH))],
        out_specs=pl.BlockSpec((tt, BRANCH), lambda b, t: (row_map(b, t), 0)),
        scratch_shapes=[pltpu.VMEM((CHUNK + SUBLANES, 2 * BRANCH), F32),
                        pltpu.VMEM((SSD_STATE, BRANCH), F32)],
        compiler_params=_cparams(("parallel", "arbitrary")),
        name="ssd",
    )(proj, proj, proj, dt_raw, conv_w, conv_b.reshape(1, -1), dtb, aneg, dsk, norm_w.reshape(1, -1))


def _rope_tables(pos_col, inv_freq_row, first_half):
    ang = pos_col * inv_freq_row
    cos = jnp.cos(ang)
    sin = jnp.sin(ang)
    return cos, jnp.where(first_half, -sin, sin)


def _rope_apply(x, cos, sin_signed, first_half):
    half = HEAD_DIM // 2
    outs = []
    for cblk in range(x.shape[1] // LANES):
        xb = x[:, cblk * LANES:(cblk + 1) * LANES]
        rot = jnp.where(first_half, pltpu.roll(xb, LANES - half, 1), pltpu.roll(xb, half, 1))
        outs.append(xb * cos + rot * sin_signed)
    return outs


def _swa_kernel(q_ref, kv_ref, g_ref, pos_ref, invf_ref, sink_ref, o_ref, prev_ref):
    blk = pl.program_id(1)

    @pl.when(blk == 0)
    def _():
        prev_ref[...] = jnp.zeros_like(prev_ref)

    row = lax.broadcasted_iota(jnp.int32, (WINDOW, WINDOW), 0)
    col = lax.broadcasted_iota(jnp.int32, (WINDOW, WINDOW), 1)
    lo = col < HEAD_DIM
    first_half = (col % HEAD_DIM) < (HEAD_DIM // 2)
    cur_ok = col <= row
    prev_ok = jnp.logical_and(col > row, blk > 0)

    cos, sin_s = _rope_tables(pos_ref[0].astype(F32), invf_ref[...], first_half)
    q_blocks = _rope_apply(q_ref[...].astype(F32) * (HEAD_DIM ** -0.5), cos, sin_s, first_half)
    kv = kv_ref[...].astype(F32)
    k_blocks = _rope_apply(kv[:, 0:KV_WIDTH], cos, sin_s, first_half)

    cur = []
    for g in range(KV_HEADS):
        kp = k_blocks[g // 2]
        vp = kv[:, KV_WIDTH + (g // 2) * LANES:KV_WIDTH + (g // 2 + 1) * LANES]
        kr = pltpu.roll(kp, HEAD_DIM, 1)
        vr = pltpu.roll(vp, HEAD_DIM, 1)
        own_lo = (g % 2) == 0
        kdup_lo = jnp.where(lo, kp if own_lo else kr, 0.0).astype(BF16)
        kdup_hi = jnp.where(lo, 0.0, kr if own_lo else kp).astype(BF16)
        vdup_lo = jnp.where(lo, vp if own_lo else vr, 0.0).astype(BF16)
        vdup_hi = jnp.where(lo, 0.0, vr if own_lo else vp).astype(BF16)
        cur.append((kdup_lo, kdup_hi, vdup_lo, vdup_hi))

    nt = (((1,), (1,)), ((), ()))
    outs = []
    for j in range(ATTN_HEADS // 2):
        g = j // 2
        qb = q_blocks[j].astype(BF16)
        acc = jnp.zeros((WINDOW, LANES), F32)
        dens = []
        for hh in range(2):
            k_cur, v_cur = cur[g][hh], cur[g][2 + hh]
            k_prev, v_prev = prev_ref[4 * g + hh], prev_ref[4 * g + 2 + hh]
            s_cur = lax.dot_general(qb, k_cur, nt, preferred_element_type=F32)
            s_prev = lax.dot_general(qb, k_prev, nt, preferred_element_type=F32)
            s_cur = jnp.where(cur_ok, s_cur, -jnp.inf)
            s_prev = jnp.where(prev_ok, s_prev, -jnp.inf)
            sink = sink_ref[:, 2 * j + hh:2 * j + hh + 1]
            mx = jnp.maximum(jnp.maximum(s_cur.max(-1, keepdims=True), s_prev.max(-1, keepdims=True)), sink)
            e_cur = jnp.exp(s_cur - mx)
            e_prev = jnp.exp(s_prev - mx)
            den = e_cur.sum(-1, keepdims=True) + e_prev.sum(-1, keepdims=True) + jnp.exp(sink - mx)
            inv = 1.0 / den
            acc = acc + jnp.dot((e_cur * inv).astype(BF16), v_cur, preferred_element_type=F32)
            acc = acc + jnp.dot((e_prev * inv).astype(BF16), v_prev, preferred_element_type=F32)
        outs.append(acc)
    o = jnp.concatenate(outs, axis=1)
    o_ref[...] = (o * _silu(g_ref[...].astype(F32))).astype(o_ref.dtype)

    for g in range(KV_HEADS):
        for i in range(4):
            prev_ref[4 * g + i] = cur[g][i]


def _swa(proj, pos3, inv_freq, sinks, bsz, seq):
    m = proj.shape[0]
    per_b = seq // WINDOW
    row_map = lambda b, t: b * per_b + t
    return pl.pallas_call(
        _swa_kernel,
        out_shape=jax.ShapeDtypeStruct((m, BRANCH), BF16),
        grid=(bsz, per_b),
        in_specs=[pl.BlockSpec((WINDOW, BRANCH), lambda b, t: (row_map(b, t), COL_Q)),
                  pl.BlockSpec((WINDOW, 2 * KV_WIDTH), lambda b, t: (row_map(b, t), COL_KV)),
                  pl.BlockSpec((WINDOW, BRANCH), lambda b, t: (row_map(b, t), COL_GB)),
                  pl.BlockSpec((1, WINDOW, 1), lambda b, t: (row_map(b, t), 0, 0)),
                  pl.BlockSpec((1, LANES), lambda b, t: (0, 0)),
                  pl.BlockSpec((1, ATTN_HEADS), lambda b, t: (0, 0))],
        out_specs=pl.BlockSpec((WINDOW, BRANCH), lambda b, t: (row_map(b, t), 0)),
        scratch_shapes=[pltpu.VMEM((4 * KV_HEADS, WINDOW, LANES), BF16)],
        compiler_params=_cparams(("parallel", "arbitrary")),
        name="swa",
    )(proj, proj, proj, pos3, inv_freq, sinks.reshape(1, -1))


def _cd_kernel(cb_ref, cc_ref, cx_ref, cg_ref, dx_ref, dg_ref, scw_ref, lcw_ref, lcb_ref, wbd_ref,
               ba_ref, bx_ref, lam_ref, yc_ref, yd_ref, cbuf_ref, dbuf_ref, h_ref, *, tt):
    pad = SUBLANES

    @pl.when(pl.program_id(1) == 0)
    def _():
        cbuf_ref[0:pad, :] = jnp.zeros((pad, BRANCH), F32)
        dbuf_ref[0:pad, :] = jnp.zeros((pad, BRANCH), F32)
        h_ref[...] = jnp.zeros_like(h_ref)

    cbuf_ref[pad:pad + tt, :] = cc_ref[...].astype(F32) * cx_ref[...].astype(F32)
    conv = scw_ref[SCONV - 1:SCONV, :] * cbuf_ref[pad:pad + tt, :]
    for k in range(SCONV - 1):
        off = pad - (SCONV - 1) + k
        conv = conv + scw_ref[k:k + 1, :] * cbuf_ref[off:off + tt, :]
    cbuf_ref[0:pad, :] = cbuf_ref[tt:tt + pad, :]
    yc_ref[...] = (cb_ref[...].astype(F32) * conv * _silu(cg_ref[...].astype(F32))).astype(yc_ref.dtype)

    dbuf_ref[pad:pad + tt, :] = dx_ref[...].astype(F32)
    xs = lcb_ref[...] + lcw_ref[LRU_CONV - 1:LRU_CONV, :] * dbuf_ref[pad:pad + tt, :]
    for k in range(LRU_CONV - 1):
        off = pad - (LRU_CONV - 1) + k
        xs = xs + lcw_ref[k:k + 1, :] * dbuf_ref[off:off + tt, :]
    dbuf_ref[0:pad, :] = dbuf_ref[tt:tt + pad, :]
    xs_b = xs.astype(BF16)
    gates = [jnp.dot(xs_b[:, q * LRU_GROUP:(q + 1) * LRU_GROUP], wbd_ref[q], preferred_element_type=F32)
             for q in range(BRANCH // LRU_GROUP)]
    pre_a = jnp.concatenate([gq[:, 0:LRU_GROUP] for gq in gates], axis=1)
    pre_x = jnp.concatenate([gq[:, LRU_GROUP:2 * LRU_GROUP] for gq in gates], axis=1)
    rg = jax.nn.sigmoid(pre_a + ba_ref[...])
    ig = jax.nn.sigmoid(pre_x + bx_ref[...])
    log_a = -LRU_C * rg * _softplus(-lam_ref[...])
    a = jnp.exp(log_a)
    u = jnp.sqrt(jnp.tanh(-log_a) * (a * a + 1.0)) * (ig * xs)

    rows = lax.broadcasted_iota(jnp.int32, (tt, BRANCH), 0)
    k = 1
    while k < tt:
        keep = rows >= k
        a_s = jnp.where(keep, pltpu.roll(a, k, 0), 1.0)
        u_s = jnp.where(keep, pltpu.roll(u, k, 0), 0.0)
        u = a * u_s + u
        a = a * a_s
        k *= 2
    h = u + a * h_ref[0:1, :]
    h_ref[...] = jnp.broadcast_to(h[tt - 1:tt, :], h_ref.shape)
    yd_ref[...] = (h * _silu(dg_ref[...].astype(F32))).astype(yd_ref.dtype)


def _block_diag_pairs(w_a, w_x):
    per = LRU_GROUP // HEAD_DIM
    eye = jnp.eye(per, dtype=w_a.dtype)

    def bd(w):
        w4 = w.reshape(LRU_BLOCKS // per, per, HEAD_DIM, HEAD_DIM)
        return jnp.einsum('qiab,ij->qiajb', w4, eye).reshape(LRU_BLOCKS // per, LRU_GROUP, LRU_GROUP)

    return jnp.concatenate([bd(w_a), bd(w_x)], axis=2).astype(BF16)


def _cd(proj, sconv_w, lru_conv_w, lru_conv_b, w_a, b_a, w_x, b_x, lam, bsz, seq, tt):
    m = proj.shape[0]
    per_b = seq // tt
    row_map = lambda b, t: b * per_b + t
    col = lambda cidx: pl.BlockSpec((tt, BRANCH), lambda b, t: (row_map(b, t), cidx))
    const = lambda shape: pl.BlockSpec(shape, lambda b, t: (0,) * len(shape))
    wbd = _block_diag_pairs(w_a, w_x)
    out_spec = pl.BlockSpec((tt, BRANCH), lambda b, t: (row_map(b, t), 0))
    return pl.pallas_call(
        functools.partial(_cd_kernel, tt=tt),
        out_shape=(jax.ShapeDtypeStruct((m, BRANCH), BF16), jax.ShapeDtypeStruct((m, BRANCH), BF16)),
        grid=(bsz, per_b),
        in_specs=[col(COL_CB), col(COL_CC), col(COL_CX), col(COL_CG), col(COL_DX), col(COL_DG),
                  const((SCONV, BRANCH)), const((LRU_CONV, BRANCH)), const((1, BRANCH)),
                  const(wbd.shape), const((1, BRANCH)), const((1, BRANCH)), const((1, BRANCH))],
        out_specs=(out_spec, out_spec),
        scratch_shapes=[pltpu.VMEM((tt + SUBLANES, BRANCH), F32),
                        pltpu.VMEM((tt + SUBLANES, BRANCH), F32),
                        pltpu.VMEM((SUBLANES, BRANCH), F32)],
        compiler_params=_cparams(("parallel", "arbitrary")),
        name="sconv_rglru",
    )(proj, proj, proj, proj, proj, proj, sconv_w, lru_conv_w, lru_conv_b.reshape(1, -1), wbd,
      b_a.reshape(1, -1), b_x.reshape(1, -1), lam.reshape(1, -1))


def _merge_kernel(ya_ref, yb_ref, yc_ref, yd_ref, lg_ref, x_ref, gate_ref, bg_ref, wb_ref, wo_ref,
                  lnw_ref, lnb_ref, o_ref):
    m = None
    for k, y_ref in enumerate((ya_ref, yb_ref, yc_ref, yd_ref)):
        gk = jax.nn.sigmoid(lg_ref[:, k * D_MODEL:(k + 1) * D_MODEL].astype(F32) + bg_ref[k:k + 1, :])
        t = gk * jnp.dot(y_ref[...], wb_ref[k], preferred_element_type=F32)
        m = t if m is None else m + t
    out = jnp.dot(m.astype(BF16), wo_ref[...], preferred_element_type=F32)
    r = ALPHA * x_ref[...] + gate_ref[0] * out
    mu = jnp.mean(r, axis=-1, keepdims=True)
    rc = r - mu
    var = jnp.mean(jnp.square(rc), axis=-1, keepdims=True)
    o_ref[...] = rc * lax.rsqrt(var + LN_EPS) * lnw_ref[...] + lnb_ref[...]


def _merge(ya, yb, yc, yd, proj, x2, ada3, b_gate, w_branch, w_out, ln_w, ln_b, seq, tm):
    m, d = x2.shape
    per_b = seq // tm
    ycol = pl.BlockSpec((tm, BRANCH), lambda i: (i, 0))
    once = pl.Buffered(1)
    return pl.pallas_call(
        _merge_kernel,
        out_shape=jax.ShapeDtypeStruct((m, d), F32),
        grid=(m // tm,),
        in_specs=[ycol, ycol, ycol, ycol,
                  pl.BlockSpec((tm, N_BRANCH * d), lambda i: (i, 0)),
                  pl.BlockSpec((tm, d), lambda i: (i, 0)),
                  pl.BlockSpec((1, 1, d), lambda i: (i // per_b, 0, 2)),
                  pl.BlockSpec((N_BRANCH, d), lambda i: (0, 0)),
                  pl.BlockSpec((N_BRANCH, BRANCH, d), lambda i: (0, 0, 0), pipeline_mode=once),
                  pl.BlockSpec((d, d), lambda i: (0, 0), pipeline_mode=once),
                  pl.BlockSpec((1, d), lambda i: (0, 0)),
                  pl.BlockSpec((1, d), lambda i: (0, 0))],
        out_specs=pl.BlockSpec((tm, d), lambda i: (i, 0)),
        compiler_params=_cparams(("parallel",)),
        name="merge",
    )(ya, yb, yc, yd, proj, x2, ada3, b_gate, w_branch, w_out,
      ln_w.reshape(1, -1), ln_b.reshape(1, -1))


def _layer(x2, pos3, c, inv_freq, p, bsz, seq):
    d = x2.shape[1]
    ada3 = _ada(c, p["w_ada"], p["b_ada"]).reshape(bsz, 1, 3 * d)
    w_main, w_dt = _rearrange_w_in(p["w_in"])
    proj, dt_raw = _inproj(x2, ada3, w_main, w_dt, seq, tm=min(1024, seq), tn=1536)
    ya = _ssd(proj, dt_raw, p["ssd_conv_w"], p["ssd_conv_b"], p["ssd_dt_bias"], p["ssd_a_log"],
              p["ssd_d"], p["ssd_norm_w"], bsz, seq, tt=min(512, seq))
    yb = _swa(proj, pos3, inv_freq, p["attn_sinks"], bsz, seq)
    yc, yd = _cd(proj, p["sconv_w"], p["lru_conv_w"], p["lru_conv_b"], p["lru_w_a"], p["lru_b_a"],
                 p["lru_w_x"], p["lru_b_x"], p["lru_lambda"], bsz, seq, tt=min(256, seq))
    return _merge(ya, yb, yc, yd, proj, x2, ada3, p["b_gate"], p["w_branch"].astype(BF16),
                  p["w_out"].astype(BF16), p["ln_w"], p["ln_b"], seq, tm=min(256, seq))


def kernel(x, c, positions, w_ada, b_ada, w_in, b_gate, ssd_conv_w, ssd_conv_b, ssd_dt_bias, ssd_a_log,
           ssd_d, ssd_norm_w, attn_sinks, sconv_w, lru_conv_w, lru_conv_b, lru_w_a, lru_b_a, lru_w_x,
           lru_b_x, lru_lambda, w_branch, w_out, ln_w, ln_b):
    bsz, seq, d = x.shape
    params = dict(w_ada=w_ada, b_ada=b_ada, w_in=w_in, b_gate=b_gate, ssd_conv_w=ssd_conv_w,
                  ssd_conv_b=ssd_conv_b, ssd_dt_bias=ssd_dt_bias, ssd_a_log=ssd_a_log, ssd_d=ssd_d,
                  ssd_norm_w=ssd_norm_w, attn_sinks=attn_sinks, sconv_w=sconv_w, lru_conv_w=lru_conv_w,
                  lru_conv_b=lru_conv_b, lru_w_a=lru_w_a, lru_b_a=lru_b_a, lru_w_x=lru_w_x,
                  lru_b_x=lru_b_x, lru_lambda=lru_lambda, w_branch=w_branch, w_out=w_out, ln_w=ln_w,
                  ln_b=ln_b)
    half = HEAD_DIM // 2
    inv_half = ROPE_THETA ** (-jnp.arange(half, dtype=F32) / half)
    inv_freq = jnp.tile(inv_half, LANES // half).reshape(1, LANES)
    pos3 = positions.reshape(bsz * seq // WINDOW, WINDOW, 1)
    x2 = x.reshape(bsz * seq, d)
    for i in range(DEPTH):
        x2 = _layer(x2, pos3, c, inv_freq, {k: v[i] for k, v in params.items()}, bsz, seq)
    return x2.reshape(bsz, seq, d)
```

```python
import functools

import jax
import jax.numpy as jnp
import numpy as np
from jax import lax
from jax.experimental import pallas as pl
from jax.experimental.pallas import tpu as pltpu

F32 = jnp.float32
BF16 = jnp.bfloat16

LANES = 128
SUBLANES = 8
VMEM_LIMIT = 56 * 1024 * 1024

D_MODEL = 2048
BRANCH = D_MODEL // 2
N_BRANCH = 4
HEAD_DIM = 64
SSD_HEADS = BRANCH // HEAD_DIM
SSD_GROUPS = 4
SSD_STATE = 128
SSD_CONV = 4
CHUNK = 128
ATTN_HEADS = BRANCH // HEAD_DIM
KV_HEADS = 4
KV_WIDTH = KV_HEADS * HEAD_DIM
WINDOW = 128
ROPE_THETA = 10000.0
SCONV = 3
LRU_BLOCKS = 16
LRU_CONV = 4
LRU_C = 8.0
LRU_GROUP = 256
LN_EPS = 1e-5
RMS_EPS = 1e-5
DEPTH = 2
ALPHA = (2.0 * DEPTH) ** 0.25

MERGE_COLS = N_BRANCH * D_MODEL
_C0 = MERGE_COLS // BRANCH
COL_Z, COL_XS, COL_BC, COL_Q, COL_GB = _C0, _C0 + 1, _C0 + 2, _C0 + 3, _C0 + 4
COL_CB, COL_CC, COL_CX, COL_CG, COL_DX, COL_DG = (_C0 + 5, _C0 + 6, _C0 + 7, _C0 + 8, _C0 + 9, _C0 + 10)
COL_KV = (MERGE_COLS + 11 * BRANCH) // (2 * KV_WIDTH)
ORIG_DT = 3 * BRANCH
ORIG_Q = ORIG_DT + SSD_HEADS
ORIG_K = ORIG_Q + BRANCH
ORIG_GB = ORIG_K + 2 * KV_WIDTH
ORIG_MERGE = ORIG_GB + 7 * BRANCH


def _rearrange_w_in(w_in):
    w_main = jnp.concatenate(
        [w_in[:, ORIG_MERGE:], w_in[:, :ORIG_DT], w_in[:, ORIG_Q:ORIG_K], w_in[:, ORIG_GB:ORIG_MERGE],
         w_in[:, ORIG_K:ORIG_GB]], axis=1).astype(BF16)
    w_dt = jnp.pad(w_in[:, ORIG_DT:ORIG_Q], ((0, 0), (0, LANES - SSD_HEADS))).astype(BF16)
    return w_main, w_dt


def _softplus(x):
    return jnp.maximum(x, 0.0) + jnp.log1p(jnp.exp(-jnp.abs(x)))


def _silu(x):
    return x * jax.nn.sigmoid(x)


def _cparams(sem):
    return pltpu.CompilerParams(dimension_semantics=sem, vmem_limit_bytes=VMEM_LIMIT)


def _ada_kernel(c_ref, w_ref, b_ref, o_ref):
    ca = _silu(c_ref[...])
    o_ref[...] = jnp.dot(ca, w_ref[...], preferred_element_type=F32,
                         precision=lax.Precision.HIGHEST) + b_ref[...]


def _ada(c, w_ada, b_ada):
    bsz, d = c.shape
    n = w_ada.shape[1]
    tn = 768
    return pl.pallas_call(
        _ada_kernel,
        out_shape=jax.ShapeDtypeStruct((bsz, n), F32),
        grid=(n // tn,),
        in_specs=[pl.BlockSpec((bsz, d), lambda j: (0, 0)),
                  pl.BlockSpec((d, tn), lambda j: (0, j)),
                  pl.BlockSpec((1, tn), lambda j: (0, j))],
        out_specs=pl.BlockSpec((bsz, tn), lambda j: (0, j)),
        compiler_params=_cparams(("arbitrary",)),
        name="ada",
    )(c, w_ada, b_ada.reshape(1, n))


def _inproj_kernel(x_ref, shift_ref, scale_ref, w_ref, wdt_ref, o_ref, dt_ref, h_ref):
    @pl.when(pl.program_id(1) == 0)
    def _():
        h = x_ref[...] * (1.0 + scale_ref[0]) + shift_ref[0]
        hb = h.astype(BF16)
        h_ref[...] = hb
        dt_ref[...] = jnp.dot(hb, wdt_ref[...], preferred_element_type=F32)

    o_ref[...] = jnp.dot(h_ref[...], w_ref[...], preferred_element_type=F32).astype(o_ref.dtype)


def _inproj(x2, ada3, w_main, w_dt, seq, tm, tn):
    m, d = x2.shape
    n = w_main.shape[1]
    per_b = seq // tm
    return pl.pallas_call(
        _inproj_kernel,
        out_shape=(jax.ShapeDtypeStruct((m, n), BF16), jax.ShapeDtypeStruct((m, LANES), F32)),
        grid=(m // tm, n // tn),
        in_specs=[pl.BlockSpec((tm, d), lambda i, j: (i, 0)),
                  pl.BlockSpec((1, 1, d), lambda i, j: (i // per_b, 0, 0)),
                  pl.BlockSpec((1, 1, d), lambda i, j: (i // per_b, 0, 1)),
                  pl.BlockSpec((d, tn), lambda i, j: (0, j)),
                  pl.BlockSpec((d, LANES), lambda i, j: (0, 0))],
        out_specs=(pl.BlockSpec((tm, tn), lambda i, j: (i, j)),
                   pl.BlockSpec((tm, LANES), lambda i, j: (i, 0))),
        scratch_shapes=[pltpu.VMEM((tm, d), BF16)],
        compiler_params=_cparams(("parallel", "arbitrary")),
        name="inproj",
    )(x2, ada3, ada3, w_main, w_dt)


def _pair_expand(col_a, col_b, lo):
    shape = (col_a.shape[0], LANES)
    return jnp.where(lo, jnp.broadcast_to(col_a, shape), jnp.broadcast_to(col_b, shape))


def _ssd_kernel(z_ref, xs_ref, bc_ref, dt_ref, cw_ref, cb_ref, dtb_ref, aneg_ref, dsk_ref, nw_ref,
                o_ref, cbuf_ref, state_ref, *, n_chunks):
    pad = SUBLANES

    @pl.when(pl.program_id(1) == 0)
    def _():
        cbuf_ref[0:pad, :] = jnp.zeros((pad, 2 * BRANCH), F32)
        state_ref[...] = jnp.zeros_like(state_ref)

    row = lax.broadcasted_iota(jnp.int32, (CHUNK, CHUNK), 0)
    col = lax.broadcasted_iota(jnp.int32, (CHUNK, CHUNK), 1)
    causal = row >= col
    tril = causal.astype(F32)
    lo = col < HEAD_DIM
    lo_row = lo[0:1, :]

    def chunk_body(ci, carry):
        r0 = pl.multiple_of(ci * CHUNK, CHUNK)
        cbuf_ref[pad:pad + CHUNK, 0:BRANCH] = xs_ref[pl.ds(r0, CHUNK), :].astype(F32)
        cbuf_ref[pad:pad + CHUNK, BRANCH:2 * BRANCH] = bc_ref[pl.ds(r0, CHUNK), :].astype(F32)
        conv = cb_ref[...] + cw_ref[SSD_CONV - 1:SSD_CONV, :] * cbuf_ref[pad:pad + CHUNK, :]
        for k in range(SSD_CONV - 1):
            off = pad - (SSD_CONV - 1) + k
            conv = conv + cw_ref[k:k + 1, :] * cbuf_ref[off:off + CHUNK, :]
        cbuf_ref[0:pad, :] = cbuf_ref[CHUNK:CHUNK + pad, :]
        xbc = _silu(conv)
        xc = xbc[:, 0:BRANCH]

        dt = _softplus(dt_ref[pl.ds(r0, CHUNK), :] + dtb_ref[...])
        da = dt * aneg_ref[...]
        cum = jnp.dot(tril, da, preferred_element_type=F32, precision=lax.Precision.HIGHEST)
        cum_t = cum.T

        ys = []
        for g in range(SSD_GROUPS):
            b_g = xbc[:, BRANCH + g * SSD_STATE:BRANCH + (g + 1) * SSD_STATE]
            c_g = xbc[:, BRANCH + (SSD_GROUPS + g) * SSD_STATE:BRANCH + (SSD_GROUPS + g + 1) * SSD_STATE]
            b_gb = b_g.astype(BF16)
            c_gb = c_g.astype(BF16)
            cb = lax.dot_general(c_gb, b_gb, (((1,), (1,)), ((), ())), preferred_element_type=F32)
            b_t = b_g.T.astype(BF16)
            for jp in range(2):
                j = 2 * g + jp
                h0, h1 = 2 * j, 2 * j + 1
                x_p = xc[:, j * LANES:(j + 1) * LANES]
                cum0 = jnp.broadcast_to(cum[:, h0:h0 + 1], (CHUNK, CHUNK))
                cum1 = jnp.broadcast_to(cum[:, h1:h1 + 1], (CHUNK, CHUNK))
                cum_p = jnp.where(lo, cum0, cum1)
                dt_p = _pair_expand(dt[:, h0:h0 + 1], dt[:, h1:h1 + 1], lo)
                xdt = x_p * dt_p
                xdt_lo = jnp.where(lo, xdt, 0.0).astype(BF16)
                xdt_hi = jnp.where(lo, 0.0, xdt).astype(BF16)
                l0 = jnp.exp(jnp.where(causal, cum0 - cum_t[h0:h0 + 1, :], -jnp.inf))
                l1 = jnp.exp(jnp.where(causal, cum1 - cum_t[h1:h1 + 1, :], -jnp.inf))
                y = jnp.dot((cb * l0).astype(BF16), xdt_lo, preferred_element_type=F32)
                y = y + jnp.dot((cb * l1).astype(BF16), xdt_hi, preferred_element_type=F32)
                st = state_ref[:, j * LANES:(j + 1) * LANES]
                y_off = jnp.dot(c_gb, st.astype(BF16), preferred_element_type=F32)
                y = y + y_off * jnp.exp(cum_p)
                y = y + x_p * dsk_ref[:, j * LANES:(j + 1) * LANES]
                ys.append(y)
                cum_last = cum_p[CHUNK - 1:CHUNK, :]
                xw = (xdt * jnp.exp(cum_last - cum_p)).astype(BF16)
                s_new = jnp.dot(b_t, xw, preferred_element_type=F32)
                state_ref[:, j * LANES:(j + 1) * LANES] = st * jnp.exp(cum_last) + s_new
        y = jnp.concatenate(ys, axis=1)
        y = y * _silu(z_ref[pl.ds(r0, CHUNK), :].astype(F32))
        y = y * lax.rsqrt(jnp.mean(jnp.square(y), axis=-1, keepdims=True) + RMS_EPS)
        o_ref[pl.ds(r0, CHUNK), :] = (y * nw_ref[...]).astype(o_ref.dtype)
        return carry

    lax.fori_loop(0, n_chunks, chunk_body, 0)


def _ssd(proj, dt_raw, conv_w, conv_b, dt_bias, a_log, d_skip, norm_w, bsz, seq, tt):
    m = proj.shape[0]
    per_b = seq // tt
    row_map = lambda b, t: b * per_b + t
    pad_heads = LANES - SSD_HEADS
    dtb = jnp.pad(dt_bias, (0, pad_heads)).reshape(1, LANES)
    aneg = jnp.pad(-jnp.exp(a_log), (0, pad_heads)).reshape(1, LANES)
    dsk = jnp.repeat(d_skip, HEAD_DIM).reshape(1, BRANCH)
    const = lambda shape: pl.BlockSpec(shape, lambda b, t: (0, 0))
    return pl.pallas_call(
        functools.partial(_ssd_kernel, n_chunks=tt // CHUNK),
        out_shape=jax.ShapeDtypeStruct((m, BRANCH), BF16),
        grid=(bsz, per_b),
        in_specs=[pl.BlockSpec((tt, BRANCH), lambda b, t: (row_map(b, t), COL_Z)),
                  pl.BlockSpec((tt, BRANCH), lambda b, t: (row_map(b, t), COL_XS)),
                  pl.BlockSpec((tt, BRANCH), lambda b, t: (row_map(b, t), COL_BC)),
                  pl.BlockSpec((tt, LANES), lambda b, t: (row_map(b, t), 0)),
                  const((SSD_CONV, 2 * BRANCH)), const((1, 2 * BRANCH)),
                  const((1, LANES)), const((1, LANES)), const((1, BRANCH)), const((1, BRANCH))],
        out_specs=pl.BlockSpec((tt, BRANCH), lambda b, t: (row_map(b, t), 0)),
        scratch_shapes=[pltpu.VMEM((CHUNK + SUBLANES, 2 * BRANCH), F32),
                        pltpu.VMEM((SSD_STATE, BRANCH), F32)],
        compiler_params=_cparams(("parallel", "arbitrary")),
        name="ssd",
    )(proj, proj, proj, dt_raw, conv_w, conv_b.reshape(1, -1), dtb, aneg, dsk, norm_w.reshape(1, -1))


def _rope_table_kernel(pos_ref, invf_ref, cos_ref, sin_ref):
    ang = pos_ref[...].astype(F32) * invf_ref[...]
    lane = lax.broadcasted_iota(jnp.int32, ang.shape, 1)
    first_half = (lane % HEAD_DIM) < (HEAD_DIM // 2)
    cos_ref[...] = jnp.cos(ang)
    sin = jnp.sin(ang)
    sin_ref[...] = jnp.where(first_half, -sin, sin)


def _rope_tables(positions, tt):
    half = HEAD_DIM // 2
    inv_half = ROPE_THETA ** (-jnp.arange(half, dtype=F32) / half)
    inv_freq = jnp.tile(inv_half, LANES // half).reshape(1, LANES)
    m = positions.size
    shape = jax.ShapeDtypeStruct((m, LANES), F32)
    return pl.pallas_call(
        _rope_table_kernel,
        out_shape=(shape, shape),
        grid=(m // tt,),
        in_specs=[pl.BlockSpec((tt, 1), lambda i: (i, 0)), pl.BlockSpec((1, LANES), lambda i: (0, 0))],
        out_specs=(pl.BlockSpec((tt, LANES), lambda i: (i, 0)), pl.BlockSpec((tt, LANES), lambda i: (i, 0))),
        compiler_params=_cparams(("parallel",)),
        name="rope_tables",
    )(positions.reshape(m, 1), inv_freq)


def _rope_apply(x, cos, sin_signed, first_half):
    half = HEAD_DIM // 2
    outs = []
    for cblk in range(x.shape[1] // LANES):
        xb = x[:, cblk * LANES:(cblk + 1) * LANES]
        rot = jnp.where(first_half, pltpu.roll(xb, LANES - half, 1), pltpu.roll(xb, half, 1))
        outs.append(xb * cos + rot * sin_signed)
    return outs


def _swa_kernel(q_ref, kv_ref, g_ref, cos_ref, sin_ref, sink_ref, o_ref,
                kcat_ref, vtcat_ref, qb_ref, s_ref, e_ref, inv_ref, *, n_blocks):
    tile = pl.program_id(1)

    @pl.when(tile == 0)
    def _():
        kcat_ref[...] = jnp.zeros_like(kcat_ref)
        vtcat_ref[...] = jnp.zeros_like(vtcat_ref)

    row = lax.broadcasted_iota(jnp.int32, (WINDOW, WINDOW), 0)
    lane = lax.broadcasted_iota(jnp.int32, (WINDOW, WINDOW), 1)
    lo = lane < HEAD_DIM
    top = row < HEAD_DIM
    first_half = (lane % HEAD_DIM) < (HEAD_DIM // 2)
    key = lax.broadcasted_iota(jnp.int32, (2 * WINDOW, WINDOW), 0)
    qry = lax.broadcasted_iota(jnp.int32, (2 * WINDOW, WINDOW), 1)
    band = jnp.logical_and(key > qry, key - WINDOW <= qry)
    nt = (((1,), (1,)), ((), ()))
    n_pairs = ATTN_HEADS // 2

    def block_body(bi, carry):
        rows = pl.ds(pl.multiple_of(bi * WINDOW, WINDOW), WINDOW)
        valid = jnp.logical_and(band, jnp.logical_or(key >= WINDOW, tile * n_blocks + bi > 0))
        cos = cos_ref[rows, :]
        sin_s = sin_ref[rows, :]
        q_blocks = _rope_apply(q_ref[rows, :].astype(F32) * (HEAD_DIM ** -0.5), cos, sin_s, first_half)
        for j in range(n_pairs):
            qb_ref[j] = q_blocks[j].astype(BF16)
        kv = kv_ref[rows, :].astype(F32)
        k_blocks = _rope_apply(kv[:, 0:KV_WIDTH], cos, sin_s, first_half)

        for g in range(KV_HEADS):
            kp = k_blocks[g // 2]
            vp = kv[:, KV_WIDTH + (g // 2) * LANES:KV_WIDTH + (g // 2 + 1) * LANES]
            kr = pltpu.roll(kp, HEAD_DIM, 1)
            vr = pltpu.roll(vp, HEAD_DIM, 1)
            own_lo = (g % 2) == 0
            kcat_ref[2 * g, WINDOW:, :] = jnp.where(lo, kp if own_lo else kr, 0.0).astype(BF16)
            kcat_ref[2 * g + 1, WINDOW:, :] = jnp.where(lo, 0.0, kr if own_lo else kp).astype(BF16)
            v_t = jnp.where(lo, vp if own_lo else vr, vr if own_lo else vp).T
            vtcat_ref[2 * g, :, WINDOW:] = jnp.where(top, v_t, 0.0).astype(BF16)
            vtcat_ref[2 * g + 1, :, WINDOW:] = jnp.where(top, 0.0, v_t).astype(BF16)

        for j in range(n_pairs):
            for hh in range(2):
                s_ref[2 * j + hh] = lax.dot_general(kcat_ref[2 * (j // 2) + hh], qb_ref[j], nt,
                                                    preferred_element_type=F32)
        for h in range(ATTN_HEADS):
            s = jnp.where(valid, s_ref[h], -jnp.inf)
            sink = sink_ref[h:h + 1, :]
            mx = jnp.maximum(s.max(0, keepdims=True), sink)
            e = jnp.exp(s - mx)
            inv_ref[h:h + 1, :] = 1.0 / (e.sum(0, keepdims=True) + jnp.exp(sink - mx))
            e_ref[h] = e.astype(BF16)
        outs = []
        for j in range(n_pairs):
            g = j // 2
            acc = jnp.dot(vtcat_ref[2 * g], e_ref[2 * j], preferred_element_type=F32)
            acc = acc + jnp.dot(vtcat_ref[2 * g + 1], e_ref[2 * j + 1], preferred_element_type=F32)
            inv = jnp.where(top, inv_ref[2 * j:2 * j + 1, :], inv_ref[2 * j + 1:2 * j + 2, :])
            outs.append((acc * inv).T)
        o = jnp.concatenate(outs, axis=1)
        o_ref[rows, :] = (o * _silu(g_ref[rows, :].astype(F32))).astype(o_ref.dtype)

        for i in range(2 * KV_HEADS):
            kcat_ref[i, :WINDOW, :] = kcat_ref[i, WINDOW:, :]
            vtcat_ref[i, :, :WINDOW] = vtcat_ref[i, :, WINDOW:]
        return carry

    lax.fori_loop(0, n_blocks, block_body, 0)


def _swa(proj, cos_t, sin_t, sinks, bsz, seq, tt):
    m = proj.shape[0]
    per_b = seq // tt
    row_map = lambda b, t: b * per_b + t
    sinks_b = jnp.broadcast_to(sinks.reshape(ATTN_HEADS, 1), (ATTN_HEADS, LANES))
    return pl.pallas_call(
        functools.partial(_swa_kernel, n_blocks=tt // WINDOW),
        out_shape=jax.ShapeDtypeStruct((m, BRANCH), BF16),
        grid=(bsz, per_b),
        in_specs=[pl.BlockSpec((tt, BRANCH), lambda b, t: (row_map(b, t), COL_Q)),
                  pl.BlockSpec((tt, 2 * KV_WIDTH), lambda b, t: (row_map(b, t), COL_KV)),
                  pl.BlockSpec((tt, BRANCH), lambda b, t: (row_map(b, t), COL_GB)),
                  pl.BlockSpec((tt, LANES), lambda b, t: (row_map(b, t), 0)),
                  pl.BlockSpec((tt, LANES), lambda b, t: (row_map(b, t), 0)),
                  pl.BlockSpec((ATTN_HEADS, LANES), lambda b, t: (0, 0))],
        out_specs=pl.BlockSpec((tt, BRANCH), lambda b, t: (row_map(b, t), 0)),
        scratch_shapes=[pltpu.VMEM((2 * KV_HEADS, 2 * WINDOW, LANES), BF16),
                        pltpu.VMEM((2 * KV_HEADS, LANES, 2 * WINDOW), BF16),
                        pltpu.VMEM((ATTN_HEADS // 2, WINDOW, LANES), BF16),
                        pltpu.VMEM((ATTN_HEADS, 2 * WINDOW, WINDOW), F32),
                        pltpu.VMEM((ATTN_HEADS, 2 * WINDOW, WINDOW), BF16),
                        pltpu.VMEM((ATTN_HEADS, LANES), F32)],
        compiler_params=_cparams(("parallel", "arbitrary")),
        name="swa",
    )(proj, proj, proj, cos_t, sin_t, sinks_b)


def _cd_kernel(cb_ref, cc_ref, cx_ref, cg_ref, dx_ref, dg_ref, scw_ref, lcw_ref, lcb_ref, wbd_ref,
               ba_ref, bx_ref, lam_ref, yc_ref, yd_ref, cbuf_ref, dbuf_ref, h_ref, *, tt):
    pad = SUBLANES

    @pl.when(pl.program_id(1) == 0)
    def _():
        cbuf_ref[0:pad, :] = jnp.zeros((pad, BRANCH), F32)
        dbuf_ref[0:pad, :] = jnp.zeros((pad, BRANCH), F32)
        h_ref[...] = jnp.zeros_like(h_ref)

    cbuf_ref[pad:pad + tt, :] = cc_ref[...].astype(F32) * cx_ref[...].astype(F32)
    conv = scw_ref[SCONV - 1:SCONV, :] * cbuf_ref[pad:pad + tt, :]
    for k in range(SCONV - 1):
        off = pad - (SCONV - 1) + k
        conv = conv + scw_ref[k:k + 1, :] * cbuf_ref[off:off + tt, :]
    cbuf_ref[0:pad, :] = cbuf_ref[tt:tt + pad, :]
    yc_ref[...] = (cb_ref[...].astype(F32) * conv * _silu(cg_ref[...].astype(F32))).astype(yc_ref.dtype)

    dbuf_ref[pad:pad + tt, :] = dx_ref[...].astype(F32)
    xs = lcb_ref[...] + lcw_ref[LRU_CONV - 1:LRU_CONV, :] * dbuf_ref[pad:pad + tt, :]
    for k in range(LRU_CONV - 1):
        off = pad - (LRU_CONV - 1) + k
        xs = xs + lcw_ref[k:k + 1, :] * dbuf_ref[off:off + tt, :]
    dbuf_ref[0:pad, :] = dbuf_ref[tt:tt + pad, :]
    xs_b = xs.astype(BF16)
    gates = [jnp.dot(xs_b[:, q * LRU_GROUP:(q + 1) * LRU_GROUP], wbd_ref[q], preferred_element_type=F32)
             for q in range(BRANCH // LRU_GROUP)]
    pre_a = jnp.concatenate([gq[:, 0:LRU_GROUP] for gq in gates], axis=1)
    pre_x = jnp.concatenate([gq[:, LRU_GROUP:2 * LRU_GROUP] for gq in gates], axis=1)
    rg = jax.nn.sigmoid(pre_a + ba_ref[...])
    ig = jax.nn.sigmoid(pre_x + bx_ref[...])
    log_a = -LRU_C * rg * _softplus(-lam_ref[...])
    a = jnp.exp(log_a)
    u = jnp.sqrt(jnp.tanh(-log_a) * (a * a + 1.0)) * (ig * xs)

    groups = tt // SUBLANES
    a3 = a.reshape(groups, SUBLANES, BRANCH)
    u3 = u.reshape(groups, SUBLANES, BRANCH)
    sub = lax.broadcasted_iota(jnp.int32, a3.shape, 1)
    k = 1
    while k < SUBLANES:
        keep = sub >= k
        a_s = jnp.where(keep, pltpu.roll(a3, k, 1), 1.0)
        u_s = jnp.where(keep, pltpu.roll(u3, k, 1), 0.0)
        u3 = a3 * u_s + u3
        a3 = a3 * a_s
        k *= 2
    carry = h_ref[0:1, :]
    hs = []
    for gi in range(groups):
        hg = u3[gi] + a3[gi] * carry
        carry = hg[SUBLANES - 1:SUBLANES, :]
        hs.append(hg)
    h_ref[...] = jnp.broadcast_to(carry, h_ref.shape)
    h = jnp.concatenate(hs, axis=0)
    yd_ref[...] = (h * _silu(dg_ref[...].astype(F32))).astype(yd_ref.dtype)


def _block_diag_pairs(w_a, w_x):
    per = LRU_GROUP // HEAD_DIM
    eye = jnp.eye(per, dtype=w_a.dtype)

    def bd(w):
        w4 = w.reshape(LRU_BLOCKS // per, per, HEAD_DIM, HEAD_DIM)
        return jnp.einsum('qiab,ij->qiajb', w4, eye).reshape(LRU_BLOCKS // per, LRU_GROUP, LRU_GROUP)

    return jnp.concatenate([bd(w_a), bd(w_x)], axis=2).astype(BF16)


def _cd(proj, sconv_w, lru_conv_w, lru_conv_b, w_a, b_a, w_x, b_x, lam, bsz, seq, tt):
    m = proj.shape[0]
    per_b = seq // tt
    row_map = lambda b, t: b * per_b + t
    col = lambda cidx: pl.BlockSpec((tt, BRANCH), lambda b, t: (row_map(b, t), cidx))
    const = lambda shape: pl.BlockSpec(shape, lambda b, t: (0,) * len(shape))
    wbd = _block_diag_pairs(w_a, w_x)
    out_spec = pl.BlockSpec((tt, BRANCH), lambda b, t: (row_map(b, t), 0))
    return pl.pallas_call(
        functools.partial(_cd_kernel, tt=tt),
        out_shape=(jax.ShapeDtypeStruct((m, BRANCH), BF16), jax.ShapeDtypeStruct((m, BRANCH), BF16)),
        grid=(bsz, per_b),
        in_specs=[col(COL_CB), col(COL_CC), col(COL_CX), col(COL_CG), col(COL_DX), col(COL_DG),
                  const((SCONV, BRANCH)), const((LRU_CONV, BRANCH)), const((1, BRANCH)),
                  const(wbd.shape), const((1, BRANCH)), const((1, BRANCH)), const((1, BRANCH))],
        out_specs=(out_spec, out_spec),
        scratch_shapes=[pltpu.VMEM((tt + SUBLANES, BRANCH), F32),
                        pltpu.VMEM((tt + SUBLANES, BRANCH), F32),
                        pltpu.VMEM((SUBLANES, BRANCH), F32)],
        compiler_params=_cparams(("parallel", "arbitrary")),
        name="sconv_rglru",
    )(proj, proj, proj, proj, proj, proj, sconv_w, lru_conv_w, lru_conv_b.reshape(1, -1), wbd,
      b_a.reshape(1, -1), b_x.reshape(1, -1), lam.reshape(1, -1))


def _merge_kernel(ya_ref, yb_ref, yc_ref, yd_ref, lg_ref, x_ref, gate_ref, bg_ref, wb_ref, wo_ref,
                  lnw_ref, lnb_ref, o_ref):
    m = None
    for k, y_ref in enumerate((ya_ref, yb_ref, yc_ref, yd_ref)):
        gk = jax.nn.sigmoid(lg_ref[:, k * D_MODEL:(k + 1) * D_MODEL].astype(F32) + bg_ref[k:k + 1, :])
        t = gk * jnp.dot(y_ref[...], wb_ref[k], preferred_element_type=F32)
        m = t if m is None else m + t
    out = jnp.dot(m.astype(BF16), wo_ref[...], preferred_element_type=F32)
    r = ALPHA * x_ref[...] + gate_ref[0] * out
    mu = jnp.mean(r, axis=-1, keepdims=True)
    rc = r - mu
    var = jnp.mean(jnp.square(rc), axis=-1, keepdims=True)
    o_ref[...] = rc * lax.rsqrt(var + LN_EPS) * lnw_ref[...] + lnb_ref[...]


def _merge(ya, yb, yc, yd, proj, x2, ada3, b_gate, w_branch, w_out, ln_w, ln_b, seq, tm):
    m, d = x2.shape
    per_b = seq // tm
    ycol = pl.BlockSpec((tm, BRANCH), lambda i: (i, 0))
    once = pl.Buffered(1)
    return pl.pallas_call(
        _merge_kernel,
        out_shape=jax.ShapeDtypeStruct((m, d), F32),
        grid=(m // tm,),
        in_specs=[ycol, ycol, ycol, ycol,
                  pl.BlockSpec((tm, N_BRANCH * d), lambda i: (i, 0)),
                  pl.BlockSpec((tm, d), lambda i: (i, 0)),
                  pl.BlockSpec((1, 1, d), lambda i: (i // per_b, 0, 2)),
                  pl.BlockSpec((N_BRANCH, d), lambda i: (0, 0)),
                  pl.BlockSpec((N_BRANCH, BRANCH, d), lambda i: (0, 0, 0), pipeline_mode=once),
                  pl.BlockSpec((d, d), lambda i: (0, 0), pipeline_mode=once),
                  pl.BlockSpec((1, d), lambda i: (0, 0)),
                  pl.BlockSpec((1, d), lambda i: (0, 0))],
        out_specs=pl.BlockSpec((tm, d), lambda i: (i, 0)),
        compiler_params=_cparams(("parallel",)),
        name="merge",
    )(ya, yb, yc, yd, proj, x2, ada3, b_gate, w_branch, w_out,
      ln_w.reshape(1, -1), ln_b.reshape(1, -1))


def _layer(x2, cos_t, sin_t, c, p, bsz, seq):
    d = x2.shape[1]
    ada3 = _ada(c, p["w_ada"], p["b_ada"]).reshape(bsz, 1, 3 * d)
    w_main, w_dt = _rearrange_w_in(p["w_in"])
    proj, dt_raw = _inproj(x2, ada3, w_main, w_dt, seq, tm=min(1024, seq), tn=1536)
    ya = _ssd(proj, dt_raw, p["ssd_conv_w"], p["ssd_conv_b"], p["ssd_dt_bias"], p["ssd_a_log"],
              p["ssd_d"], p["ssd_norm_w"], bsz, seq, tt=min(512, seq))
    yb = _swa(proj, cos_t, sin_t, p["attn_sinks"], bsz, seq, tt=min(512, seq))
    yc, yd = _cd(proj, p["sconv_w"], p["lru_conv_w"], p["lru_conv_b"], p["lru_w_a"], p["lru_b_a"],
                 p["lru_w_x"], p["lru_b_x"], p["lru_lambda"], bsz, seq, tt=min(256, seq))
    return _merge(ya, yb, yc, yd, proj, x2, ada3, p["b_gate"], p["w_branch"].astype(BF16),
                  p["w_out"].astype(BF16), p["ln_w"], p["ln_b"], seq, tm=min(256, seq))


def kernel(x, c, positions, w_ada, b_ada, w_in, b_gate, ssd_conv_w, ssd_conv_b, ssd_dt_bias, ssd_a_log,
           ssd_d, ssd_norm_w, attn_sinks, sconv_w, lru_conv_w, lru_conv_b, lru_w_a, lru_b_a, lru_w_x,
           lru_b_x, lru_lambda, w_branch, w_out, ln_w, ln_b):
    bsz, seq, d = x.shape
    params = dict(w_ada=w_ada, b_ada=b_ada, w_in=w_in, b_gate=b_gate, ssd_conv_w=ssd_conv_w,
                  ssd_conv_b=ssd_conv_b, ssd_dt_bias=ssd_dt_bias, ssd_a_log=ssd_a_log, ssd_d=ssd_d,
                  ssd_norm_w=ssd_norm_w, attn_sinks=attn_sinks, sconv_w=sconv_w, lru_conv_w=lru_conv_w,
                  lru_conv_b=lru_conv_b, lru_w_a=lru_w_a, lru_b_a=lru_b_a, lru_w_x=lru_w_x,
                  lru_b_x=lru_b_x, lru_lambda=lru_lambda, w_branch=w_branch, w_out=w_out, ln_w=ln_w,
                  ln_b=ln_b)
    cos_t, sin_t = _rope_tables(positions, tt=min(2048, bsz * seq))
    x2 = x.reshape(bsz * seq, d)
    for i in range(DEPTH):
        x2 = _layer(x2, cos_t, sin_t, c, {k: v[i] for k, v in params.items()}, bsz, seq)
    return x2.reshape(bsz, seq, d)
```

```python
import functools

import jax
import jax.numpy as jnp
from jax import lax
from jax.experimental import pallas as pl
from jax.experimental.pallas import tpu as pltpu

F32 = jnp.float32
BF16 = jnp.bfloat16

LANES = 128
SUBLANES = 8
VMEM_LIMIT = 56 * 1024 * 1024

D_MODEL = 2048
BRANCH = D_MODEL // 2
N_BRANCH = 4
HEAD_DIM = 64
SSD_HEADS = BRANCH // HEAD_DIM
SSD_GROUPS = 4
SSD_STATE = 128
SSD_CONV = 4
CHUNK = 128
ATTN_HEADS = BRANCH // HEAD_DIM
KV_HEADS = 4
KV_WIDTH = KV_HEADS * HEAD_DIM
WINDOW = 128
ROPE_THETA = 10000.0
SCONV = 3
LRU_BLOCKS = 16
LRU_CONV = 4
LRU_C = 8.0
LRU_GROUP = 256
LN_EPS = 1e-5
RMS_EPS = 1e-5
DEPTH = 2
ALPHA = (2.0 * DEPTH) ** 0.25
LOG2E = 1.4426950408889634

MERGE_COLS = N_BRANCH * D_MODEL
_C0 = MERGE_COLS // BRANCH
COL_Z, COL_XS, COL_BC, COL_Q, COL_GB = _C0, _C0 + 1, _C0 + 2, _C0 + 3, _C0 + 4
COL_CB, COL_CC, COL_CX, COL_CG, COL_DX, COL_DG = (_C0 + 5, _C0 + 6, _C0 + 7, _C0 + 8, _C0 + 9, _C0 + 10)
COL_KV = (MERGE_COLS + 11 * BRANCH) // (2 * KV_WIDTH)
ORIG_DT = 3 * BRANCH
ORIG_Q = ORIG_DT + SSD_HEADS
ORIG_K = ORIG_Q + BRANCH
ORIG_GB = ORIG_K + 2 * KV_WIDTH
ORIG_MERGE = ORIG_GB + 7 * BRANCH

MIX_TILE = 256
MERGE_COL_BLOCKS = 4


def _rearrange_w_in(w_in):
    w_main = jnp.concatenate(
        [w_in[..., ORIG_MERGE:], w_in[..., :ORIG_DT], w_in[..., ORIG_Q:ORIG_K], w_in[..., ORIG_GB:ORIG_MERGE],
         w_in[..., ORIG_K:ORIG_GB]], axis=-1).astype(BF16)
    pad = [(0, 0)] * (w_in.ndim - 1) + [(0, LANES - SSD_HEADS)]
    w_dt = jnp.pad(w_in[..., ORIG_DT:ORIG_Q], pad).astype(BF16)
    return w_main, w_dt


def _softplus(x):
    return jnp.maximum(x, 0.0) + jnp.log1p(jnp.exp(-jnp.abs(x)))


def _silu(x):
    return x * jax.nn.sigmoid(x)


def _cparams(sem):
    return pltpu.CompilerParams(dimension_semantics=sem, vmem_limit_bytes=VMEM_LIMIT)


def _ada_kernel(c_ref, w_ref, b_ref, o_ref):
    ca = _silu(c_ref[...])
    o_ref[...] = jnp.dot(ca, w_ref[...], preferred_element_type=F32,
                         precision=lax.Precision.HIGHEST) + b_ref[...]


def _ada(c, w_ada, b_ada, layer):
    bsz, d = c.shape
    n = w_ada.shape[-1]
    tn = 768
    return pl.pallas_call(
        _ada_kernel,
        out_shape=jax.ShapeDtypeStruct((bsz, n), F32),
        grid=(n // tn,),
        in_specs=[pl.BlockSpec((bsz, d), lambda j: (0, 0)),
                  pl.BlockSpec((None, d, tn), lambda j: (layer, 0, j)),
                  pl.BlockSpec((None, 1, tn), lambda j: (layer, 0, j))],
        out_specs=pl.BlockSpec((bsz, tn), lambda j: (0, j)),
        compiler_params=_cparams(("arbitrary",)),
        name="ada",
    )(c, w_ada, b_ada.reshape(b_ada.shape[0], 1, n))


MODE_PLAIN, MODE_SILU, MODE_GATE = 0, 1, 2


def _inproj_kernel(x_ref, shift_ref, scale_ref, w_ref, wdt_ref, bias_ref, mode_ref, o_ref, dt_ref, h_ref):
    @pl.when(pl.program_id(1) == 0)
    def _():
        h = x_ref[...] * (1.0 + scale_ref[0]) + shift_ref[0]
        hb = h.astype(BF16)
        h_ref[...] = hb
        dt_ref[...] = jnp.dot(hb, wdt_ref[...], preferred_element_type=F32)

    acc = jnp.dot(h_ref[...], w_ref[...], preferred_element_type=F32)
    sig = jax.nn.sigmoid(acc + bias_ref[...])
    mode = mode_ref[...]
    out = jnp.where(mode == MODE_GATE, sig, jnp.where(mode == MODE_SILU, acc * sig, acc))
    o_ref[...] = out.astype(o_ref.dtype)


def _inproj_columns(b_gate):
    n = MERGE_COLS + 11 * BRANCH + 2 * KV_WIDTH
    bias = jnp.zeros((n,), F32).at[:MERGE_COLS].set(b_gate.reshape(-1))
    mode = jnp.zeros((n,), jnp.int32).at[:MERGE_COLS].set(MODE_GATE)
    for cidx in (COL_Z, COL_GB, COL_CG, COL_DG):
        mode = mode.at[cidx * BRANCH:(cidx + 1) * BRANCH].set(MODE_SILU)
    return bias.reshape(1, n), mode.reshape(1, n)


def _inproj(x2, ada3, w_main, w_dt, b_gate, layer, seq, tm, tn):
    m, d = x2.shape
    n = w_main.shape[-1]
    per_b = seq // tm
    bias, mode = _inproj_columns(b_gate)
    return pl.pallas_call(
        _inproj_kernel,
        out_shape=(jax.ShapeDtypeStruct((m, n), BF16), jax.ShapeDtypeStruct((m, LANES), F32)),
        grid=(m // tm, n // tn),
        in_specs=[pl.BlockSpec((tm, d), lambda i, j: (i, 0)),
                  pl.BlockSpec((1, 1, d), lambda i, j: (i // per_b, 0, 0)),
                  pl.BlockSpec((1, 1, d), lambda i, j: (i // per_b, 0, 1)),
                  pl.BlockSpec((None, d, tn), lambda i, j: (layer, 0, j)),
                  pl.BlockSpec((None, d, LANES), lambda i, j: (layer, 0, 0)),
                  pl.BlockSpec((1, tn), lambda i, j: (0, j)),
                  pl.BlockSpec((1, tn), lambda i, j: (0, j))],
        out_specs=(pl.BlockSpec((tm, tn), lambda i, j: (i, j)),
                   pl.BlockSpec((tm, LANES), lambda i, j: (i, 0))),
        scratch_shapes=[pltpu.VMEM((tm, d), BF16)],
        compiler_params=_cparams(("parallel", "arbitrary")),
        name="inproj",
    )(x2, ada3, ada3, w_main, w_dt, bias, mode)


def _pair_expand(col_a, col_b, lo):
    shape = (col_a.shape[0], LANES)
    return jnp.where(lo, jnp.broadcast_to(col_a, shape), jnp.broadcast_to(col_b, shape))


def _shift_matrix(n_shifts):
    r = lax.broadcasted_iota(jnp.int32, (n_shifts * CHUNK, 2 * CHUNK), 0)
    c = lax.broadcasted_iota(jnp.int32, (n_shifts * CHUNK, 2 * CHUNK), 1)
    return jnp.where(c == CHUNK + (r % CHUNK) - r // CHUNK, 1.0, 0.0).astype(BF16)


def _shifted_taps(xbuf, shift_mat, n_taps):
    sh = jnp.dot(shift_mat, xbuf, preferred_element_type=F32)
    return [sh[s * CHUNK:(s + 1) * CHUNK, :] for s in range(n_taps)]


def _ssd_chunk(r0, z_ref, xs_ref, bc_ref, dt_ref, cw_ref, cb_ref, dtb_ref, aneg_ref, dsk_ref, nw_ref,
               xbuf_ref, state_ref, causal, lo, shift_mat):
    rows = slice(r0, r0 + CHUNK)
    tril = causal.astype(F32)
    xbuf_ref[CHUNK:, 0:BRANCH] = xs_ref[rows, :]
    xbuf_ref[CHUNK:, BRANCH:2 * BRANCH] = bc_ref[rows, :]
    taps = _shifted_taps(xbuf_ref[...], shift_mat, SSD_CONV)
    conv = cb_ref[...]
    for s in range(SSD_CONV):
        conv = conv + cw_ref[SSD_CONV - 1 - s:SSD_CONV - s, :] * taps[s]
    xbuf_ref[:CHUNK, :] = xbuf_ref[CHUNK:, :]
    xbc = _silu(conv)
    xc = xbc[:, 0:BRANCH]

    dt = _softplus(dt_ref[rows, :] + dtb_ref[...])
    da = dt * aneg_ref[...]
    cum = LOG2E * jnp.dot(tril, da, preferred_element_type=F32, precision=lax.Precision.HIGHEST)
    cum_t = cum.T

    ys = []
    for g in range(SSD_GROUPS):
        b_g = xbc[:, BRANCH + g * SSD_STATE:BRANCH + (g + 1) * SSD_STATE]
        c_g = xbc[:, BRANCH + (SSD_GROUPS + g) * SSD_STATE:BRANCH + (SSD_GROUPS + g + 1) * SSD_STATE]
        b_gb = b_g.astype(BF16)
        c_gb = c_g.astype(BF16)
        cb = lax.dot_general(c_gb, b_gb, (((1,), (1,)), ((), ())), preferred_element_type=F32)
        b_t = b_g.T.astype(BF16)
        for jp in range(2):
            j = 2 * g + jp
            h0, h1 = 2 * j, 2 * j + 1
            x_p = xc[:, j * LANES:(j + 1) * LANES]
            cum0 = jnp.broadcast_to(cum[:, h0:h0 + 1], (CHUNK, CHUNK))
            cum1 = jnp.broadcast_to(cum[:, h1:h1 + 1], (CHUNK, CHUNK))
            cum_p = jnp.where(lo, cum0, cum1)
            dt_p = _pair_expand(dt[:, h0:h0 + 1], dt[:, h1:h1 + 1], lo)
            xdt = x_p * dt_p
            xdt_b = xdt.astype(BF16)
            l0 = jnp.exp2(jnp.where(causal, cum0 - cum_t[h0:h0 + 1, :], -jnp.inf))
            l1 = jnp.exp2(jnp.where(causal, cum1 - cum_t[h1:h1 + 1, :], -jnp.inf))
            y = jnp.where(lo, jnp.dot((cb * l0).astype(BF16), xdt_b, preferred_element_type=F32),
                          jnp.dot((cb * l1).astype(BF16), xdt_b, preferred_element_type=F32))
            st = state_ref[:, j * LANES:(j + 1) * LANES]
            y_off = jnp.dot(c_gb, st.astype(BF16), preferred_element_type=F32)
            y = y + y_off * jnp.exp2(cum_p)
            y = y + x_p * dsk_ref[:, j * LANES:(j + 1) * LANES]
            ys.append(y)
            cum_last = cum_p[CHUNK - 1:CHUNK, :]
            xw = (xdt * jnp.exp2(cum_last - cum_p)).astype(BF16)
            s_new = jnp.dot(b_t, xw, preferred_element_type=F32)
            state_ref[:, j * LANES:(j + 1) * LANES] = st * jnp.exp2(cum_last) + s_new
            yield
    y = jnp.concatenate(ys, axis=1)
    y = y * z_ref[rows, :].astype(F32)
    y = y * lax.rsqrt(jnp.mean(jnp.square(y), axis=-1, keepdims=True) + RMS_EPS)
    return (y * nw_ref[...]).astype(BF16)


def _rope_table_kernel(pos_ref, invf_ref, cos_ref, sin_ref):
    ang = pos_ref[...].astype(F32) * invf_ref[...]
    lane = lax.broadcasted_iota(jnp.int32, ang.shape, 1)
    first_half = (lane % HEAD_DIM) < (HEAD_DIM // 2)
    cos_ref[...] = jnp.cos(ang)
    sin = jnp.sin(ang)
    sin_ref[...] = jnp.where(first_half, -sin, sin)


def _rope_tables(positions, tt):
    half = HEAD_DIM // 2
    inv_half = ROPE_THETA ** (-jnp.arange(half, dtype=F32) / half)
    inv_freq = jnp.tile(inv_half, LANES // half).reshape(1, LANES)
    m = positions.size
    shape = jax.ShapeDtypeStruct((m, LANES), F32)
    return pl.pallas_call(
        _rope_table_kernel,
        out_shape=(shape, shape),
        grid=(m // tt,),
        in_specs=[pl.BlockSpec((tt, 1), lambda i: (i, 0)), pl.BlockSpec((1, LANES), lambda i: (0, 0))],
        out_specs=(pl.BlockSpec((tt, LANES), lambda i: (i, 0)), pl.BlockSpec((tt, LANES), lambda i: (i, 0))),
        compiler_params=_cparams(("parallel",)),
        name="rope_tables",
    )(positions.reshape(m, 1), inv_freq)


def _rope_apply(x, cos, sin_signed, first_half):
    half = HEAD_DIM // 2
    outs = []
    for cblk in range(x.shape[1] // LANES):
        xb = x[:, cblk * LANES:(cblk + 1) * LANES]
        rot = jnp.where(first_half, pltpu.roll(xb, LANES - half, 1), pltpu.roll(xb, half, 1))
        outs.append(xb * cos + rot * sin_signed)
    return outs


def _swa_block(r0, has_prev, q_ref, kv_ref, g_ref, cos_ref, sin_ref, sink_ref,
               kcat_ref, vtcat_ref, qb_ref, s_ref, e_ref, inv_ref):
    rows = slice(r0, r0 + WINDOW)
    row = lax.broadcasted_iota(jnp.int32, (WINDOW, WINDOW), 0)
    lane = lax.broadcasted_iota(jnp.int32, (WINDOW, WINDOW), 1)
    lo = lane < HEAD_DIM
    top = row < HEAD_DIM
    first_half = (lane % HEAD_DIM) < (HEAD_DIM // 2)
    key = lax.broadcasted_iota(jnp.int32, (2 * WINDOW, WINDOW), 0)
    qry = lax.broadcasted_iota(jnp.int32, (2 * WINDOW, WINDOW), 1)
    valid = jnp.logical_and(key > qry, key - WINDOW <= qry)
    if has_prev is not True:
        valid = jnp.logical_and(valid, jnp.logical_or(key >= WINDOW, has_prev))
    nt = (((1,), (1,)), ((), ()))
    n_pairs = ATTN_HEADS // 2

    cos = cos_ref[rows, :]
    sin_s = sin_ref[rows, :]
    q_blocks = _rope_apply(q_ref[rows, :].astype(F32) * (LOG2E * HEAD_DIM ** -0.5), cos, sin_s, first_half)
    for j in range(n_pairs):
        qb_ref[j] = q_blocks[j].astype(BF16)
    kv = kv_ref[rows, :].astype(F32)
    k_blocks = _rope_apply(kv[:, 0:KV_WIDTH], cos, sin_s, first_half)

    for g in range(KV_HEADS):
        kp = k_blocks[g // 2]
        vp = kv[:, KV_WIDTH + (g // 2) * LANES:KV_WIDTH + (g // 2 + 1) * LANES]
        kr = pltpu.roll(kp, HEAD_DIM, 1)
        vr = pltpu.roll(vp, HEAD_DIM, 1)
        own_lo = (g % 2) == 0
        kcat_ref[2 * g, WINDOW:, :] = jnp.where(lo, kp if own_lo else kr, 0.0).astype(BF16)
        kcat_ref[2 * g + 1, WINDOW:, :] = jnp.where(lo, 0.0, kr if own_lo else kp).astype(BF16)
        v_t = jnp.where(lo, vp if own_lo else vr, vr if own_lo else vp).T
        vtcat_ref[2 * g, :, WINDOW:] = jnp.where(top, v_t, 0.0).astype(BF16)
        vtcat_ref[2 * g + 1, :, WINDOW:] = jnp.where(top, 0.0, v_t).astype(BF16)

    for j in range(n_pairs):
        for hh in range(2):
            s_ref[2 * j + hh] = lax.dot_general(kcat_ref[2 * (j // 2) + hh], qb_ref[j], nt,
                                                preferred_element_type=F32)
    for h in range(ATTN_HEADS):
        s = jnp.where(valid, s_ref[h], -jnp.inf)
        sink = LOG2E * sink_ref[h:h + 1, :]
        mx = jnp.maximum(s.max(0, keepdims=True), sink)
        e = jnp.exp2(s - mx)
        inv_ref[h:h + 1, :] = 1.0 / (e.sum(0, keepdims=True) + jnp.exp2(sink - mx))
        e_ref[h] = e.astype(BF16)
        yield
    outs = []
    for j in range(n_pairs):
        g = j // 2
        acc = jnp.dot(vtcat_ref[2 * g], e_ref[2 * j], preferred_element_type=F32)
        acc = acc + jnp.dot(vtcat_ref[2 * g + 1], e_ref[2 * j + 1], preferred_element_type=F32)
        inv = jnp.where(top, inv_ref[2 * j:2 * j + 1, :], inv_ref[2 * j + 1:2 * j + 2, :])
        outs.append((acc * inv).T)
        yield
    o = jnp.concatenate(outs, axis=1)

    for i in range(2 * KV_HEADS):
        kcat_ref[i, :WINDOW, :] = kcat_ref[i, WINDOW:, :]
        vtcat_ref[i, :, :WINDOW] = vtcat_ref[i, :, WINDOW:]
    return (o * g_ref[rows, :].astype(F32)).astype(BF16)


def _cd_tile(cb_ref, cc_ref, cx_ref, cg_ref, dx_ref, dg_ref, scw_ref, lcw_ref, lcb_ref, wbd_ref,
             ba_ref, bx_ref, lam_ref, cbuf_ref, dbuf_ref, h_ref, shift_mat, tt):
    pad = SUBLANES
    cbuf_ref[pad:pad + tt, :] = cc_ref[...].astype(F32) * cx_ref[...].astype(F32)
    conv = scw_ref[SCONV - 1:SCONV, :] * cbuf_ref[pad:pad + tt, :]
    for k in range(SCONV - 1):
        off = pad - (SCONV - 1) + k
        conv = conv + scw_ref[k:k + 1, :] * cbuf_ref[off:off + tt, :]
    cbuf_ref[0:pad, :] = cbuf_ref[tt:tt + pad, :]
    yc = (cb_ref[...].astype(F32) * conv * cg_ref[...].astype(F32)).astype(BF16)
    yield

    dbuf_ref[CHUNK:, :] = dx_ref[...]
    parts = []
    for hf in range(tt // CHUNK):
        taps = _shifted_taps(dbuf_ref[hf * CHUNK:(hf + 2) * CHUNK, :], shift_mat, LRU_CONV)
        part = lcb_ref[...]
        for s in range(LRU_CONV):
            part = part + lcw_ref[LRU_CONV - 1 - s:LRU_CONV - s, :] * taps[s]
        parts.append(part)
    xs = jnp.concatenate(parts, axis=0)
    dbuf_ref[:CHUNK, :] = dbuf_ref[tt:, :]
    xs_b = xs.astype(BF16)
    yield
    gates = [jnp.dot(xs_b[:, q * LRU_GROUP:(q + 1) * LRU_GROUP], wbd_ref[q], preferred_element_type=F32)
             for q in range(BRANCH // LRU_GROUP)]
    pre_a = jnp.concatenate([gq[:, 0:LRU_GROUP] for gq in gates], axis=1)
    pre_x = jnp.concatenate([gq[:, LRU_GROUP:2 * LRU_GROUP] for gq in gates], axis=1)
    rg = jax.nn.sigmoid(pre_a + ba_ref[...])
    ig = jax.nn.sigmoid(pre_x + bx_ref[...])
    log_a = -LRU_C * rg * _softplus(-lam_ref[...])
    a = jnp.exp(log_a)
    u = jnp.sqrt(jnp.tanh(-log_a) * (a * a + 1.0)) * (ig * xs)
    yield

    groups = tt // SUBLANES
    a3 = a.reshape(groups, SUBLANES, BRANCH)
    u3 = u.reshape(groups, SUBLANES, BRANCH)
    sub = lax.broadcasted_iota(jnp.int32, a3.shape, 1)
    k = 1
    while k < SUBLANES:
        keep = sub >= k
        a_s = jnp.where(keep, pltpu.roll(a3, k, 1), 1.0)
        u_s = jnp.where(keep, pltpu.roll(u3, k, 1), 0.0)
        u3 = a3 * u_s + u3
        a3 = a3 * a_s
        k *= 2
    yield
    carry = h_ref[0:1, :]
    hs = []
    for gi in range(groups):
        hg = u3[gi] + a3[gi] * carry
        carry = hg[SUBLANES - 1:SUBLANES, :]
        hs.append(hg)
    h_ref[...] = jnp.broadcast_to(carry, h_ref.shape)
    h = jnp.concatenate(hs, axis=0)
    return yc, (h * dg_ref[...].astype(F32)).astype(BF16)


def _block_diag_pairs(w_a, w_x):
    per = LRU_GROUP // HEAD_DIM
    eye = jnp.eye(per, dtype=w_a.dtype)

    def bd(w):
        w4 = w.reshape(LRU_BLOCKS // per, per, HEAD_DIM, HEAD_DIM)
        return jnp.einsum('qiab,ij->qiajb', w4, eye).reshape(LRU_BLOCKS // per, LRU_GROUP, LRU_GROUP)

    return jnp.concatenate([bd(w_a), bd(w_x)], axis=2).astype(BF16)


CD_STAGES = 4


def _trace_round_robin(streams):
    gens = [g for g, _ in streams]
    totals = [n + 1 for _, n in streams]
    done = [0] * len(gens)
    results = [None] * len(gens)
    alive = set(range(len(gens)))
    while alive:
        i = min(alive, key=lambda s: (done[s] / totals[s], s))
        try:
            next(gens[i])
            done[i] += 1
        except StopIteration as stop:
            results[i] = stop.value
            alive.remove(i)
    return results
def _mix_kernel(z_ref, xs_ref, bc_ref, q_ref, gb_ref, cb_ref, cc_ref, cx_ref, cg_ref, dx_ref, dg_ref,
                kv_ref, lg_ref, dt_ref, cos_ref, sin_ref,
                a_cw_ref, a_cb_ref, a_dtb_ref, a_neg_ref, a_dsk_ref, a_nw_ref, sink_ref,
                scw_ref, lcw_ref, lcb_ref, wbd_ref, ba_ref, bx_ref, lam_ref, wb_ref,
                m_ref,
                a_xbuf_ref, a_state_ref, kcat_ref, vtcat_ref, qb_ref, s_ref, e_ref, inv_ref,
                c_buf_ref, d_buf_ref, d_h_ref, y_ref, macc_ref, *, tt, per_b):
    step = pl.program_id(0)
    tile = step % per_b

    @pl.when(step == 0)
    def _():
        y_ref[...] = jnp.zeros_like(y_ref)

    @pl.when(tile == 0)
    def _():
        a_xbuf_ref[:CHUNK, :] = jnp.zeros((CHUNK, 2 * BRANCH), BF16)
        a_state_ref[...] = jnp.zeros_like(a_state_ref)
        kcat_ref[...] = jnp.zeros_like(kcat_ref)
        vtcat_ref[...] = jnp.zeros_like(vtcat_ref)
        c_buf_ref[0:SUBLANES, :] = jnp.zeros((SUBLANES, BRANCH), F32)
        d_buf_ref[:CHUNK, :] = jnp.zeros((CHUNK, BRANCH), BF16)
        d_h_ref[...] = jnp.zeros_like(d_h_ref)

    row = lax.broadcasted_iota(jnp.int32, (CHUNK, CHUNK), 0)
    col = lax.broadcasted_iota(jnp.int32, (CHUNK, CHUNK), 1)
    causal = row >= col
    lo = col < HEAD_DIM
    shift_mat = _shift_matrix(max(SSD_CONV, LRU_CONV))
    wcol = D_MODEL // MERGE_COL_BLOCKS

    def merge_stream():
        for nb in range(MERGE_COL_BLOCKS):
            cols = slice(nb * wcol, (nb + 1) * wcol)
            for k in range(N_BRANCH):
                gk = lg_ref[:, k * D_MODEL + nb * wcol:k * D_MODEL + (nb + 1) * wcol].astype(F32)
                t = gk * jnp.dot(y_ref[k], wb_ref[k, :, cols], preferred_element_type=F32)
                if k == 0:
                    macc_ref[:, cols] = t
                elif k < N_BRANCH - 1:
                    macc_ref[:, cols] += t
                else:
                    m_ref[:, cols] = (macc_ref[:, cols] + t).astype(m_ref.dtype)
                yield

    def ssd_stream():
        outs = []
        for ci in range(tt // CHUNK):
            outs.append((yield from _ssd_chunk(
                ci * CHUNK, z_ref, xs_ref, bc_ref, dt_ref, a_cw_ref, a_cb_ref, a_dtb_ref, a_neg_ref,
                a_dsk_ref, a_nw_ref, a_xbuf_ref, a_state_ref, causal, lo, shift_mat)))
        return jnp.concatenate(outs, axis=0)

    def swa_stream():
        outs = []
        for bi in range(tt // WINDOW):
            outs.append((yield from _swa_block(
                bi * WINDOW, True if bi > 0 else tile > 0, q_ref, kv_ref, gb_ref, cos_ref, sin_ref,
                sink_ref, kcat_ref, vtcat_ref, qb_ref, s_ref, e_ref, inv_ref)))
        return jnp.concatenate(outs, axis=0)

    def cd_stream():
        return (yield from _cd_tile(cb_ref, cc_ref, cx_ref, cg_ref, dx_ref, dg_ref, scw_ref, lcw_ref, lcb_ref,
                                    wbd_ref, ba_ref, bx_ref, lam_ref, c_buf_ref, d_buf_ref, d_h_ref,
                                    shift_mat, tt))

    n_chunks = tt // CHUNK
    _, ya, yb, (yc, yd) = _trace_round_robin(
        [(merge_stream(), MERGE_COL_BLOCKS * N_BRANCH), (ssd_stream(), n_chunks * SSD_HEADS // 2),
         (swa_stream(), n_chunks * (ATTN_HEADS + ATTN_HEADS // 2)), (cd_stream(), CD_STAGES)])
    for k, y in enumerate((ya, yb, yc, yd)):
        y_ref[k] = y


def _mix(proj, dt_raw, cos_t, sin_t, p, w_branch, layer, bsz, seq, tt):
    m = proj.shape[0]
    per_b = seq // tt
    n_tiles = bsz * per_b
    cur = lambda i: jnp.minimum(i, n_tiles - 1)
    prev = lambda i: jnp.maximum(i - 1, 0)
    col = lambda cidx: pl.BlockSpec((tt, BRANCH), lambda i: (cur(i), cidx))
    lane_blk = pl.BlockSpec((tt, LANES), lambda i: (cur(i), 0))
    const = lambda shape: pl.BlockSpec(shape, lambda i: (0,) * len(shape))
    pad_heads = LANES - SSD_HEADS
    dtb = jnp.pad(p["ssd_dt_bias"], (0, pad_heads)).reshape(1, LANES)
    aneg = jnp.pad(-jnp.exp(p["ssd_a_log"]), (0, pad_heads)).reshape(1, LANES)
    dsk = jnp.repeat(p["ssd_d"], HEAD_DIM).reshape(1, BRANCH)
    sinks_b = jnp.broadcast_to(p["attn_sinks"].reshape(ATTN_HEADS, 1), (ATTN_HEADS, LANES))
    wbd = _block_diag_pairs(p["lru_w_a"], p["lru_w_x"])
    vec = lambda a: a.reshape(1, -1)
    return pl.pallas_call(
        functools.partial(_mix_kernel, tt=tt, per_b=per_b),
        out_shape=jax.ShapeDtypeStruct((m, D_MODEL), BF16),
        grid=(n_tiles + 1,),
        in_specs=[col(COL_Z), col(COL_XS), col(COL_BC), col(COL_Q), col(COL_GB), col(COL_CB), col(COL_CC),
                  col(COL_CX), col(COL_CG), col(COL_DX), col(COL_DG),
                  pl.BlockSpec((tt, 2 * KV_WIDTH), lambda i: (cur(i), COL_KV)),
                  pl.BlockSpec((tt, MERGE_COLS), lambda i: (prev(i), 0)),
                  lane_blk, lane_blk, lane_blk,
                  const((SSD_CONV, 2 * BRANCH)), const((1, 2 * BRANCH)), const((1, LANES)), const((1, LANES)),
                  const((1, BRANCH)), const((1, BRANCH)), const((ATTN_HEADS, LANES)),
                  const((SCONV, BRANCH)), const((LRU_CONV, BRANCH)), const((1, BRANCH)), const(wbd.shape),
                  const((1, BRANCH)), const((1, BRANCH)), const((1, BRANCH)),
                  pl.BlockSpec((None, N_BRANCH, BRANCH, D_MODEL), lambda i: (layer, 0, 0, 0),
                               pipeline_mode=pl.Buffered(1))],
        out_specs=pl.BlockSpec((tt, D_MODEL), lambda i: (prev(i), 0)),
        scratch_shapes=[pltpu.VMEM((2 * CHUNK, 2 * BRANCH), BF16),
                        pltpu.VMEM((SSD_STATE, BRANCH), F32),
                        pltpu.VMEM((2 * KV_HEADS, 2 * WINDOW, LANES), BF16),
                        pltpu.VMEM((2 * KV_HEADS, LANES, 2 * WINDOW), BF16),
                        pltpu.VMEM((ATTN_HEADS // 2, WINDOW, LANES), BF16),
                        pltpu.VMEM((ATTN_HEADS, 2 * WINDOW, WINDOW), F32),
                        pltpu.VMEM((ATTN_HEADS, 2 * WINDOW, WINDOW), BF16),
                        pltpu.VMEM((ATTN_HEADS, LANES), F32),
                        pltpu.VMEM((tt + SUBLANES, BRANCH), F32),
                        pltpu.VMEM((CHUNK + tt, BRANCH), BF16),
                        pltpu.VMEM((SUBLANES, BRANCH), F32),
                        pltpu.VMEM((N_BRANCH, tt, BRANCH), BF16),
                        pltpu.VMEM((tt, D_MODEL), F32)],
        compiler_params=_cparams(("arbitrary",)),
        name="mix",
    )(*([proj] * 13), dt_raw, cos_t, sin_t,
      p["ssd_conv_w"], vec(p["ssd_conv_b"]), dtb, aneg, dsk, vec(p["ssd_norm_w"]), sinks_b,
      p["sconv_w"], p["lru_conv_w"], vec(p["lru_conv_b"]), wbd, vec(p["lru_b_a"]), vec(p["lru_b_x"]),
      vec(p["lru_lambda"]), w_branch)


def _outproj_kernel(m_ref, x_ref, gate_ref, wo_ref, lnw_ref, lnb_ref, o_ref):
    out = jnp.dot(m_ref[...], wo_ref[...], preferred_element_type=F32)
    r = ALPHA * x_ref[...] + gate_ref[0] * out
    mu = jnp.mean(r, axis=-1, keepdims=True)
    rc = r - mu
    var = jnp.mean(jnp.square(rc), axis=-1, keepdims=True)
    o_ref[...] = rc * lax.rsqrt(var + LN_EPS) * lnw_ref[...] + lnb_ref[...]


def _outproj(mm, x2, ada3, w_out, ln_w, ln_b, layer, seq, tm):
    m, d = x2.shape
    per_b = seq // tm
    return pl.pallas_call(
        _outproj_kernel,
        out_shape=jax.ShapeDtypeStruct((m, d), F32),
        grid=(m // tm,),
        in_specs=[pl.BlockSpec((tm, d), lambda i: (i, 0)),
                  pl.BlockSpec((tm, d), lambda i: (i, 0)),
                  pl.BlockSpec((1, 1, d), lambda i: (i // per_b, 0, 2)),
                  pl.BlockSpec((None, d, d), lambda i: (layer, 0, 0), pipeline_mode=pl.Buffered(1)),
                  pl.BlockSpec((1, d), lambda i: (0, 0)),
                  pl.BlockSpec((1, d), lambda i: (0, 0))],
        out_specs=pl.BlockSpec((tm, d), lambda i: (i, 0)),
        compiler_params=_cparams(("parallel",)),
        name="outproj",
    )(mm, x2, ada3, w_out, ln_w.reshape(1, -1), ln_b.reshape(1, -1))


def kernel(x, c, positions, w_ada, b_ada, w_in, b_gate, ssd_conv_w, ssd_conv_b, ssd_dt_bias, ssd_a_log,
           ssd_d, ssd_norm_w, attn_sinks, sconv_w, lru_conv_w, lru_conv_b, lru_w_a, lru_b_a, lru_w_x,
           lru_b_x, lru_lambda, w_branch, w_out, ln_w, ln_b):
    bsz, seq, d = x.shape
    small = dict(b_gate=b_gate, ssd_conv_w=ssd_conv_w, ssd_conv_b=ssd_conv_b, ssd_dt_bias=ssd_dt_bias,
                 ssd_a_log=ssd_a_log, ssd_d=ssd_d, ssd_norm_w=ssd_norm_w, attn_sinks=attn_sinks,
                 sconv_w=sconv_w, lru_conv_w=lru_conv_w, lru_conv_b=lru_conv_b, lru_w_a=lru_w_a,
                 lru_b_a=lru_b_a, lru_w_x=lru_w_x, lru_b_x=lru_b_x, lru_lambda=lru_lambda)
    w_main, w_dt = _rearrange_w_in(w_in)
    w_branch_b = w_branch.astype(BF16)
    w_out_b = w_out.astype(BF16)
    cos_t, sin_t = _rope_tables(positions, tt=min(2048, bsz * seq))
    x2 = x.reshape(bsz * seq, d)
    for layer in range(DEPTH):
        p = {k: v[layer] for k, v in small.items()}
        ada3 = _ada(c, w_ada, b_ada, layer).reshape(bsz, 1, 3 * d)
        proj, dt_raw = _inproj(x2, ada3, w_main, w_dt, p["b_gate"], layer, seq, tm=min(1024, seq), tn=1536)
        mm = _mix(proj, dt_raw, cos_t, sin_t, p, w_branch_b, layer, bsz, seq, tt=min(MIX_TILE, seq))
        x2 = _outproj(mm, x2, ada3, w_out_b, ln_w[layer], ln_b[layer], layer, seq, tm=min(512, seq))
    return x2.reshape(bsz, seq, d)
```

```python
import functools

import jax
import jax.numpy as jnp
from jax import lax
from jax.experimental import pallas as pl
from jax.experimental.pallas import tpu as pltpu

F32 = jnp.float32
BF16 = jnp.bfloat16

LANES = 128
SUBLANES = 8
VMEM_LIMIT = 56 * 1024 * 1024

D_MODEL = 2048
BRANCH = D_MODEL // 2
N_BRANCH = 4
HEAD_DIM = 64
SSD_HEADS = BRANCH // HEAD_DIM
SSD_GROUPS = 4
SSD_STATE = 128
SSD_CONV = 4
CHUNK = 128
ATTN_HEADS = BRANCH // HEAD_DIM
KV_HEADS = 4
KV_WIDTH = KV_HEADS * HEAD_DIM
WINDOW = 128
ROPE_THETA = 10000.0
SCONV = 3
LRU_BLOCKS = 16
LRU_CONV = 4
LRU_C = 8.0
LRU_GROUP = 256
LN_EPS = 1e-5
RMS_EPS = 1e-5
DEPTH = 2
ALPHA = (2.0 * DEPTH) ** 0.25
LOG2E = 1.4426950408889634

MERGE_COLS = N_BRANCH * D_MODEL
_C0 = MERGE_COLS // BRANCH
COL_Z, COL_XS, COL_BC, COL_Q, COL_GB = _C0, _C0 + 1, _C0 + 2, _C0 + 3, _C0 + 4
COL_CB, COL_CC, COL_CX, COL_CG, COL_DX, COL_DG = (_C0 + 5, _C0 + 6, _C0 + 7, _C0 + 8, _C0 + 9, _C0 + 10)
COL_KV = (MERGE_COLS + 11 * BRANCH) // (2 * KV_WIDTH)
ORIG_DT = 3 * BRANCH
ORIG_Q = ORIG_DT + SSD_HEADS
ORIG_K = ORIG_Q + BRANCH
ORIG_GB = ORIG_K + 2 * KV_WIDTH
ORIG_MERGE = ORIG_GB + 7 * BRANCH


def _rearrange_w_in(w_in):
    w_main = jnp.concatenate(
        [w_in[..., ORIG_MERGE:], w_in[..., :ORIG_DT], w_in[..., ORIG_Q:ORIG_K], w_in[..., ORIG_GB:ORIG_MERGE],
         w_in[..., ORIG_K:ORIG_GB]], axis=-1).astype(BF16)
    pad = [(0, 0)] * (w_in.ndim - 1) + [(0, LANES - SSD_HEADS)]
    w_dt = jnp.pad(w_in[..., ORIG_DT:ORIG_Q], pad).astype(BF16)
    return w_main, w_dt


def _softplus(x):
    return jnp.maximum(x, 0.0) + jnp.log1p(jnp.exp(-jnp.abs(x)))


def _silu(x):
    return x * jax.nn.sigmoid(x)


def _cparams(sem):
    return pltpu.CompilerParams(dimension_semantics=sem, vmem_limit_bytes=VMEM_LIMIT)


def _ada_kernel(c_ref, w_ref, b_ref, o_ref):
    ca = _silu(c_ref[...])
    o_ref[...] = jnp.dot(ca, w_ref[...], preferred_element_type=F32,
                         precision=lax.Precision.HIGHEST) + b_ref[...]


def _ada(c, w_ada, b_ada, layer):
    bsz, d = c.shape
    n = w_ada.shape[-1]
    tn = 768
    return pl.pallas_call(
        _ada_kernel,
        out_shape=jax.ShapeDtypeStruct((bsz, n), F32),
        grid=(n // tn,),
        in_specs=[pl.BlockSpec((bsz, d), lambda j: (0, 0)),
                  pl.BlockSpec((None, d, tn), lambda j: (layer, 0, j)),
                  pl.BlockSpec((None, 1, tn), lambda j: (layer, 0, j))],
        out_specs=pl.BlockSpec((bsz, tn), lambda j: (0, j)),
        compiler_params=_cparams(("arbitrary",)),
        name="ada",
    )(c, w_ada, b_ada.reshape(b_ada.shape[0], 1, n))


def _inproj_kernel(x_ref, shift_ref, scale_ref, w_ref, wdt_ref, o_ref, dt_ref, h_ref):
    @pl.when(pl.program_id(1) == 0)
    def _():
        h = x_ref[...] * (1.0 + scale_ref[0]) + shift_ref[0]
        hb = h.astype(BF16)
        h_ref[...] = hb
        dt_ref[...] = jnp.dot(hb, wdt_ref[...], preferred_element_type=F32)

    o_ref[...] = jnp.dot(h_ref[...], w_ref[...], preferred_element_type=F32).astype(o_ref.dtype)


def _inproj(x2, ada3, w_main, w_dt, layer, seq, tm, tn):
    m, d = x2.shape
    n = w_main.shape[-1]
    per_b = seq // tm
    return pl.pallas_call(
        _inproj_kernel,
        out_shape=(jax.ShapeDtypeStruct((m, n), BF16), jax.ShapeDtypeStruct((m, LANES), F32)),
        grid=(m // tm, n // tn),
        in_specs=[pl.BlockSpec((tm, d), lambda i, j: (i, 0)),
                  pl.BlockSpec((1, 1, d), lambda i, j: (i // per_b, 0, 0)),
                  pl.BlockSpec((1, 1, d), lambda i, j: (i // per_b, 0, 1)),
                  pl.BlockSpec((None, d, tn), lambda i, j: (layer, 0, j)),
                  pl.BlockSpec((None, d, LANES), lambda i, j: (layer, 0, 0))],
        out_specs=(pl.BlockSpec((tm, tn), lambda i, j: (i, j)),
                   pl.BlockSpec((tm, LANES), lambda i, j: (i, 0))),
        scratch_shapes=[pltpu.VMEM((tm, d), BF16)],
        compiler_params=_cparams(("parallel", "arbitrary")),
        name="inproj",
    )(x2, ada3, ada3, w_main, w_dt)


def _shift_matrix(n_shifts):
    r = lax.broadcasted_iota(jnp.int32, (n_shifts * CHUNK, 2 * CHUNK), 0)
    c = lax.broadcasted_iota(jnp.int32, (n_shifts * CHUNK, 2 * CHUNK), 1)
    return jnp.where(c == CHUNK + (r % CHUNK) - r // CHUNK, 1.0, 0.0).astype(BF16)


def _shifted_taps(xbuf, shift_mat, n_taps):
    sh = jnp.dot(shift_mat, xbuf, preferred_element_type=F32)
    return [sh[s * CHUNK:(s + 1) * CHUNK, :] for s in range(n_taps)]


def _pair_expand(col_a, col_b, lo):
    shape = (col_a.shape[0], LANES)
    return jnp.where(lo, jnp.broadcast_to(col_a, shape), jnp.broadcast_to(col_b, shape))


def _ssd_kernel(z_ref, xs_ref, bc_ref, dt_ref, cw_ref, cb_ref, dtb_ref, aneg_ref, dsk_ref, nw_ref,
                o_ref, xbuf_ref, state_ref, *, n_chunks):
    @pl.when(pl.program_id(1) == 0)
    def _():
        xbuf_ref[:CHUNK, :] = jnp.zeros((CHUNK, 2 * BRANCH), BF16)
        state_ref[...] = jnp.zeros_like(state_ref)

    row = lax.broadcasted_iota(jnp.int32, (CHUNK, CHUNK), 0)
    col = lax.broadcasted_iota(jnp.int32, (CHUNK, CHUNK), 1)
    causal = row >= col
    tril = causal.astype(F32)
    lo = col < HEAD_DIM
    shift_mat = _shift_matrix(SSD_CONV)

    def chunk_body(ci, carry):
        rows = pl.ds(pl.multiple_of(ci * CHUNK, CHUNK), CHUNK)
        xbuf_ref[CHUNK:, 0:BRANCH] = xs_ref[rows, :]
        xbuf_ref[CHUNK:, BRANCH:2 * BRANCH] = bc_ref[rows, :]
        taps = _shifted_taps(xbuf_ref[...], shift_mat, SSD_CONV)
        conv = cb_ref[...]
        for s in range(SSD_CONV):
            conv = conv + cw_ref[SSD_CONV - 1 - s:SSD_CONV - s, :] * taps[s]
        xbuf_ref[:CHUNK, :] = xbuf_ref[CHUNK:, :]
        xbc = _silu(conv)
        xc = xbc[:, 0:BRANCH]

        dt = _softplus(dt_ref[rows, :] + dtb_ref[...])
        da = dt * aneg_ref[...]
        cum = LOG2E * jnp.dot(tril, da, preferred_element_type=F32, precision=lax.Precision.HIGHEST)
        cum_t = cum.T

        ys = []
        for g in range(SSD_GROUPS):
            b_g = xbc[:, BRANCH + g * SSD_STATE:BRANCH + (g + 1) * SSD_STATE]
            c_g = xbc[:, BRANCH + (SSD_GROUPS + g) * SSD_STATE:BRANCH + (SSD_GROUPS + g + 1) * SSD_STATE]
            b_gb = b_g.astype(BF16)
            c_gb = c_g.astype(BF16)
            cb = lax.dot_general(c_gb, b_gb, (((1,), (1,)), ((), ())), preferred_element_type=F32)
            b_t = b_g.T.astype(BF16)
            for jp in range(2):
                j = 2 * g + jp
                h0, h1 = 2 * j, 2 * j + 1
                x_p = xc[:, j * LANES:(j + 1) * LANES]
                cum0 = jnp.broadcast_to(cum[:, h0:h0 + 1], (CHUNK, CHUNK))
                cum1 = jnp.broadcast_to(cum[:, h1:h1 + 1], (CHUNK, CHUNK))
                cum_p = jnp.where(lo, cum0, cum1)
                dt_p = _pair_expand(dt[:, h0:h0 + 1], dt[:, h1:h1 + 1], lo)
                xdt = x_p * dt_p
                xdt_b = xdt.astype(BF16)
                l0 = jnp.exp2(jnp.where(causal, cum0 - cum_t[h0:h0 + 1, :], -jnp.inf))
                l1 = jnp.exp2(jnp.where(causal, cum1 - cum_t[h1:h1 + 1, :], -jnp.inf))
                y = jnp.where(lo, jnp.dot((cb * l0).astype(BF16), xdt_b, preferred_element_type=F32),
                              jnp.dot((cb * l1).astype(BF16), xdt_b, preferred_element_type=F32))
                st = state_ref[:, j * LANES:(j + 1) * LANES]
                y_off = jnp.dot(c_gb, st.astype(BF16), preferred_element_type=F32)
                y = y + y_off * jnp.exp2(cum_p)
                y = y + x_p * dsk_ref[:, j * LANES:(j + 1) * LANES]
                ys.append(y)
                cum_last = cum_p[CHUNK - 1:CHUNK, :]
                xw = (xdt * jnp.exp2(cum_last - cum_p)).astype(BF16)
                s_new = jnp.dot(b_t, xw, preferred_element_type=F32)
                state_ref[:, j * LANES:(j + 1) * LANES] = st * jnp.exp2(cum_last) + s_new
        y = jnp.concatenate(ys, axis=1)
        y = y * _silu(z_ref[rows, :].astype(F32))
        y = y * lax.rsqrt(jnp.mean(jnp.square(y), axis=-1, keepdims=True) + RMS_EPS)
        o_ref[rows, :] = (y * nw_ref[...]).astype(o_ref.dtype)
        return carry

    lax.fori_loop(0, n_chunks, chunk_body, 0, unroll=2)


def _ssd(proj, dt_raw, p, bsz, seq, tt):
    m = proj.shape[0]
    per_b = seq // tt
    row_map = lambda b, t: b * per_b + t
    pad_heads = LANES - SSD_HEADS
    dtb = jnp.pad(p["ssd_dt_bias"], (0, pad_heads)).reshape(1, LANES)
    aneg = jnp.pad(-jnp.exp(p["ssd_a_log"]), (0, pad_heads)).reshape(1, LANES)
    dsk = jnp.repeat(p["ssd_d"], HEAD_DIM).reshape(1, BRANCH)
    const = lambda shape: pl.BlockSpec(shape, lambda b, t: (0, 0))
    col = lambda cidx: pl.BlockSpec((tt, BRANCH), lambda b, t: (row_map(b, t), cidx))
    return pl.pallas_call(
        functools.partial(_ssd_kernel, n_chunks=tt // CHUNK),
        out_shape=jax.ShapeDtypeStruct((m, BRANCH), BF16),
        grid=(bsz, per_b),
        in_specs=[col(COL_Z), col(COL_XS), col(COL_BC),
                  pl.BlockSpec((tt, LANES), lambda b, t: (row_map(b, t), 0)),
                  const((SSD_CONV, 2 * BRANCH)), const((1, 2 * BRANCH)),
                  const((1, LANES)), const((1, LANES)), const((1, BRANCH)), const((1, BRANCH))],
        out_specs=pl.BlockSpec((tt, BRANCH), lambda b, t: (row_map(b, t), 0)),
        scratch_shapes=[pltpu.VMEM((2 * CHUNK, 2 * BRANCH), BF16),
                        pltpu.VMEM((SSD_STATE, BRANCH), F32)],
        compiler_params=_cparams(("parallel", "arbitrary")),
        name="ssd",
    )(proj, proj, proj, dt_raw, p["ssd_conv_w"], p["ssd_conv_b"].reshape(1, -1), dtb, aneg, dsk,
      p["ssd_norm_w"].reshape(1, -1))


def _rope_table_kernel(pos_ref, invf_ref, cos_ref, sin_ref):
    ang = pos_ref[...].astype(F32) * invf_ref[...]
    lane = lax.broadcasted_iota(jnp.int32, ang.shape, 1)
    first_half = (lane % HEAD_DIM) < (HEAD_DIM // 2)
    cos_ref[...] = jnp.cos(ang)
    sin = jnp.sin(ang)
    sin_ref[...] = jnp.where(first_half, -sin, sin)


def _rope_tables(positions, tt):
    half = HEAD_DIM // 2
    inv_half = ROPE_THETA ** (-jnp.arange(half, dtype=F32) / half)
    inv_freq = jnp.tile(inv_half, LANES // half).reshape(1, LANES)
    m = positions.size
    shape = jax.ShapeDtypeStruct((m, LANES), F32)
    return pl.pallas_call(
        _rope_table_kernel,
        out_shape=(shape, shape),
        grid=(m // tt,),
        in_specs=[pl.BlockSpec((tt, 1), lambda i: (i, 0)), pl.BlockSpec((1, LANES), lambda i: (0, 0))],
        out_specs=(pl.BlockSpec((tt, LANES), lambda i: (i, 0)), pl.BlockSpec((tt, LANES), lambda i: (i, 0))),
        compiler_params=_cparams(("parallel",)),
        name="rope_tables",
    )(positions.reshape(m, 1), inv_freq)


def _rope_apply(x, cos, sin_signed, first_half):
    half = HEAD_DIM // 2
    outs = []
    for cblk in range(x.shape[1] // LANES):
        xb = x[:, cblk * LANES:(cblk + 1) * LANES]
        rot = jnp.where(first_half, pltpu.roll(xb, LANES - half, 1), pltpu.roll(xb, half, 1))
        outs.append(xb * cos + rot * sin_signed)
    return outs


def _swa_kernel(q_ref, kv_ref, g_ref, cos_ref, sin_ref, sink_ref, o_ref,
                kcat_ref, vtcat_ref, qb_ref, s_ref, e_ref, inv_ref, *, n_blocks):
    tile = pl.program_id(1)

    @pl.when(tile == 0)
    def _():
        kcat_ref[...] = jnp.zeros_like(kcat_ref)
        vtcat_ref[...] = jnp.zeros_like(vtcat_ref)

    row = lax.broadcasted_iota(jnp.int32, (WINDOW, WINDOW), 0)
    lane = lax.broadcasted_iota(jnp.int32, (WINDOW, WINDOW), 1)
    lo = lane < HEAD_DIM
    top = row < HEAD_DIM
    first_half = (lane % HEAD_DIM) < (HEAD_DIM // 2)
    key = lax.broadcasted_iota(jnp.int32, (2 * WINDOW, WINDOW), 0)
    qry = lax.broadcasted_iota(jnp.int32, (2 * WINDOW, WINDOW), 1)
    band = jnp.logical_and(key > qry, key - WINDOW <= qry)
    nt = (((1,), (1,)), ((), ()))
    n_pairs = ATTN_HEADS // 2

    def block_body(bi, carry):
        rows = pl.ds(pl.multiple_of(bi * WINDOW, WINDOW), WINDOW)
        valid = jnp.logical_and(band, jnp.logical_or(key >= WINDOW, tile * n_blocks + bi > 0))
        cos = cos_ref[rows, :]
        sin_s = sin_ref[rows, :]
        q_blocks = _rope_apply(q_ref[rows, :].astype(F32) * (LOG2E * HEAD_DIM ** -0.5), cos, sin_s,
                               first_half)
        for j in range(n_pairs):
            qb_ref[j] = q_blocks[j].astype(BF16)
        kv = kv_ref[rows, :].astype(F32)
        k_blocks = _rope_apply(kv[:, 0:KV_WIDTH], cos, sin_s, first_half)

        for g in range(KV_HEADS):
            kp = k_blocks[g // 2]
            vp = kv[:, KV_WIDTH + (g // 2) * LANES:KV_WIDTH + (g // 2 + 1) * LANES]
            kr = pltpu.roll(kp, HEAD_DIM, 1)
            vr = pltpu.roll(vp, HEAD_DIM, 1)
            own_lo = (g % 2) == 0
            kcat_ref[2 * g, WINDOW:, :] = jnp.where(lo, kp if own_lo else kr, 0.0).astype(BF16)
            kcat_ref[2 * g + 1, WINDOW:, :] = jnp.where(lo, 0.0, kr if own_lo else kp).astype(BF16)
            v_t = jnp.where(lo, vp if own_lo else vr, vr if own_lo else vp).T
            vtcat_ref[2 * g, :, WINDOW:] = jnp.where(top, v_t, 0.0).astype(BF16)
            vtcat_ref[2 * g + 1, :, WINDOW:] = jnp.where(top, 0.0, v_t).astype(BF16)

        for j in range(n_pairs):
            for hh in range(2):
                s_ref[2 * j + hh] = lax.dot_general(kcat_ref[2 * (j // 2) + hh], qb_ref[j], nt,
                                                    preferred_element_type=F32)
        for h in range(ATTN_HEADS):
            s = jnp.where(valid, s_ref[h], -jnp.inf)
            sink = LOG2E * sink_ref[h:h + 1, :]
            mx = jnp.maximum(s.max(0, keepdims=True), sink)
            e = jnp.exp2(s - mx)
            inv_ref[h:h + 1, :] = 1.0 / (e.sum(0, keepdims=True) + jnp.exp2(sink - mx))
            e_ref[h] = e.astype(BF16)
        outs = []
        for j in range(n_pairs):
            g = j // 2
            acc = jnp.dot(vtcat_ref[2 * g], e_ref[2 * j], preferred_element_type=F32)
            acc = acc + jnp.dot(vtcat_ref[2 * g + 1], e_ref[2 * j + 1], preferred_element_type=F32)
            inv = jnp.where(top, inv_ref[2 * j:2 * j + 1, :], inv_ref[2 * j + 1:2 * j + 2, :])
            outs.append((acc * inv).T)
        o = jnp.concatenate(outs, axis=1)
        o_ref[rows, :] = (o * _silu(g_ref[rows, :].astype(F32))).astype(o_ref.dtype)

        for i in range(2 * KV_HEADS):
            kcat_ref[i, :WINDOW, :] = kcat_ref[i, WINDOW:, :]
            vtcat_ref[i, :, :WINDOW] = vtcat_ref[i, :, WINDOW:]
        return carry

    lax.fori_loop(0, n_blocks, block_body, 0, unroll=2)


def _swa(proj, cos_t, sin_t, sinks, bsz, seq, tt):
    m = proj.shape[0]
    per_b = seq // tt
    row_map = lambda b, t: b * per_b + t
    sinks_b = jnp.broadcast_to(sinks.reshape(ATTN_HEADS, 1), (ATTN_HEADS, LANES))
    return pl.pallas_call(
        functools.partial(_swa_kernel, n_blocks=tt // WINDOW),
        out_shape=jax.ShapeDtypeStruct((m, BRANCH), BF16),
        grid=(bsz, per_b),
        in_specs=[pl.BlockSpec((tt, BRANCH), lambda b, t: (row_map(b, t), COL_Q)),
                  pl.BlockSpec((tt, 2 * KV_WIDTH), lambda b, t: (row_map(b, t), COL_KV)),
                  pl.BlockSpec((tt, BRANCH), lambda b, t: (row_map(b, t), COL_GB)),
                  pl.BlockSpec((tt, LANES), lambda b, t: (row_map(b, t), 0)),
                  pl.BlockSpec((tt, LANES), lambda b, t: (row_map(b, t), 0)),
                  pl.BlockSpec((ATTN_HEADS, LANES), lambda b, t: (0, 0))],
        out_specs=pl.BlockSpec((tt, BRANCH), lambda b, t: (row_map(b, t), 0)),
        scratch_shapes=[pltpu.VMEM((2 * KV_HEADS, 2 * WINDOW, LANES), BF16),
                        pltpu.VMEM((2 * KV_HEADS, LANES, 2 * WINDOW), BF16),
                        pltpu.VMEM((ATTN_HEADS // 2, WINDOW, LANES), BF16),
                        pltpu.VMEM((ATTN_HEADS, 2 * WINDOW, WINDOW), F32),
                        pltpu.VMEM((ATTN_HEADS, 2 * WINDOW, WINDOW), BF16),
                        pltpu.VMEM((ATTN_HEADS, LANES), F32)],
        compiler_params=_cparams(("parallel", "arbitrary")),
        name="swa",
    )(proj, proj, proj, cos_t, sin_t, sinks_b)


def _cd_kernel(cb_ref, cc_ref, cx_ref, cg_ref, dx_ref, dg_ref, scw_ref, lcw_ref, lcb_ref, wbd_ref,
               ba_ref, bx_ref, lam_ref, yc_ref, yd_ref, cbuf_ref, dbuf_ref, h_ref, *, tt):
    pad = SUBLANES

    @pl.when(pl.program_id(1) == 0)
    def _():
        cbuf_ref[0:pad, :] = jnp.zeros((pad, BRANCH), F32)
        dbuf_ref[0:pad, :] = jnp.zeros((pad, BRANCH), F32)
        h_ref[...] = jnp.zeros_like(h_ref)

    cbuf_ref[pad:pad + tt, :] = cc_ref[...].astype(F32) * cx_ref[...].astype(F32)
    conv = scw_ref[SCONV - 1:SCONV, :] * cbuf_ref[pad:pad + tt, :]
    for k in range(SCONV - 1):
        off = pad - (SCONV - 1) + k
        conv = conv + scw_ref[k:k + 1, :] * cbuf_ref[off:off + tt, :]
    cbuf_ref[0:pad, :] = cbuf_ref[tt:tt + pad, :]
    yc_ref[...] = (cb_ref[...].astype(F32) * conv * _silu(cg_ref[...].astype(F32))).astype(yc_ref.dtype)

    dbuf_ref[pad:pad + tt, :] = dx_ref[...].astype(F32)
    xs = lcb_ref[...] + lcw_ref[LRU_CONV - 1:LRU_CONV, :] * dbuf_ref[pad:pad + tt, :]
    for k in range(LRU_CONV - 1):
        off = pad - (LRU_CONV - 1) + k
        xs = xs + lcw_ref[k:k + 1, :] * dbuf_ref[off:off + tt, :]
    dbuf_ref[0:pad, :] = dbuf_ref[tt:tt + pad, :]
    xs_b = xs.astype(BF16)
    gates = [jnp.dot(xs_b[:, q * LRU_GROUP:(q + 1) * LRU_GROUP], wbd_ref[q], preferred_element_type=F32)
             for q in range(BRANCH // LRU_GROUP)]
    pre_a = jnp.concatenate([gq[:, 0:LRU_GROUP] for gq in gates], axis=1)
    pre_x = jnp.concatenate([gq[:, LRU_GROUP:2 * LRU_GROUP] for gq in gates], axis=1)
    rg = jax.nn.sigmoid(pre_a + ba_ref[...])
    ig = jax.nn.sigmoid(pre_x + bx_ref[...])
    log_a = -LRU_C * rg * _softplus(-lam_ref[...])
    a = jnp.exp(log_a)
    u = jnp.sqrt(jnp.tanh(-log_a) * (a * a + 1.0)) * (ig * xs)

    groups = tt // SUBLANES
    a3 = a.reshape(groups, SUBLANES, BRANCH)
    u3 = u.reshape(groups, SUBLANES, BRANCH)
    sub = lax.broadcasted_iota(jnp.int32, a3.shape, 1)
    k = 1
    while k < SUBLANES:
        keep = sub >= k
        a_s = jnp.where(keep, pltpu.roll(a3, k, 1), 1.0)
        u_s = jnp.where(keep, pltpu.roll(u3, k, 1), 0.0)
        u3 = a3 * u_s + u3
        a3 = a3 * a_s
        k *= 2
    carry = h_ref[0:1, :]
    hs = []
    for gi in range(groups):
        hg = u3[gi] + a3[gi] * carry
        carry = hg[SUBLANES - 1:SUBLANES, :]
        hs.append(hg)
    h_ref[...] = jnp.broadcast_to(carry, h_ref.shape)
    h = jnp.concatenate(hs, axis=0)
    yd_ref[...] = (h * _silu(dg_ref[...].astype(F32))).astype(yd_ref.dtype)


def _block_diag_pairs(w_a, w_x):
    per = LRU_GROUP // HEAD_DIM
    eye = jnp.eye(per, dtype=w_a.dtype)

    def bd(w):
        w4 = w.reshape(LRU_BLOCKS // per, per, HEAD_DIM, HEAD_DIM)
        return jnp.einsum('qiab,ij->qiajb', w4, eye).reshape(LRU_BLOCKS // per, LRU_GROUP, LRU_GROUP)

    return jnp.concatenate([bd(w_a), bd(w_x)], axis=2).astype(BF16)


def _cd(proj, p, bsz, seq, tt):
    m = proj.shape[0]
    per_b = seq // tt
    row_map = lambda b, t: b * per_b + t
    col = lambda cidx: pl.BlockSpec((tt, BRANCH), lambda b, t: (row_map(b, t), cidx))
    const = lambda shape: pl.BlockSpec(shape, lambda b, t: (0,) * len(shape))
    wbd = _block_diag_pairs(p["lru_w_a"], p["lru_w_x"])
    out_spec = pl.BlockSpec((tt, BRANCH), lambda b, t: (row_map(b, t), 0))
    vec = lambda a: a.reshape(1, -1)
    return pl.pallas_call(
        functools.partial(_cd_kernel, tt=tt),
        out_shape=(jax.ShapeDtypeStruct((m, BRANCH), BF16), jax.ShapeDtypeStruct((m, BRANCH), BF16)),
        grid=(bsz, per_b),
        in_specs=[col(COL_CB), col(COL_CC), col(COL_CX), col(COL_CG), col(COL_DX), col(COL_DG),
                  const((SCONV, BRANCH)), const((LRU_CONV, BRANCH)), const((1, BRANCH)),
                  const(wbd.shape), const((1, BRANCH)), const((1, BRANCH)), const((1, BRANCH))],
        out_specs=(out_spec, out_spec),
        scratch_shapes=[pltpu.VMEM((tt + SUBLANES, BRANCH), F32),
                        pltpu.VMEM((tt + SUBLANES, BRANCH), F32),
                        pltpu.VMEM((SUBLANES, BRANCH), F32)],
        compiler_params=_cparams(("parallel", "arbitrary")),
        name="sconv_rglru",
    )(proj, proj, proj, proj, proj, proj, p["sconv_w"], p["lru_conv_w"], vec(p["lru_conv_b"]), wbd,
      vec(p["lru_b_a"]), vec(p["lru_b_x"]), vec(p["lru_lambda"]))


def _merge_kernel(ya_ref, yb_ref, yc_ref, yd_ref, lg_ref, x_ref, gate_ref, bg_ref, wb_ref, wo_ref,
                  lnw_ref, lnb_ref, o_ref):
    m = None
    for k, y_ref in enumerate((ya_ref, yb_ref, yc_ref, yd_ref)):
        gk = jax.nn.sigmoid(lg_ref[:, k * D_MODEL:(k + 1) * D_MODEL].astype(F32) + bg_ref[k:k + 1, :])
        t = gk * jnp.dot(y_ref[...], wb_ref[k], preferred_element_type=F32)
        m = t if m is None else m + t
    out = jnp.dot(m.astype(BF16), wo_ref[...], preferred_element_type=F32)
    r = ALPHA * x_ref[...] + gate_ref[0] * out
    mu = jnp.mean(r, axis=-1, keepdims=True)
    rc = r - mu
    var = jnp.mean(jnp.square(rc), axis=-1, keepdims=True)
    o_ref[...] = rc * lax.rsqrt(var + LN_EPS) * lnw_ref[...] + lnb_ref[...]


def _merge(ya, yb, yc, yd, proj, x2, ada3, b_gate, w_branch, w_out, ln_w, ln_b, layer, seq, tm):
    m, d = x2.shape
    per_b = seq // tm
    ycol = pl.BlockSpec((tm, BRANCH), lambda i: (i, 0))
    once = pl.Buffered(1)
    return pl.pallas_call(
        _merge_kernel,
        out_shape=jax.ShapeDtypeStruct((m, d), F32),
        grid=(m // tm,),
        in_specs=[ycol, ycol, ycol, ycol,
                  pl.BlockSpec((tm, N_BRANCH * d), lambda i: (i, 0)),
                  pl.BlockSpec((tm, d), lambda i: (i, 0)),
                  pl.BlockSpec((1, 1, d), lambda i: (i // per_b, 0, 2)),
                  pl.BlockSpec((N_BRANCH, d), lambda i: (0, 0)),
                  pl.BlockSpec((None, N_BRANCH, BRANCH, d), lambda i: (layer, 0, 0, 0), pipeline_mode=once),
                  pl.BlockSpec((None, d, d), lambda i: (layer, 0, 0), pipeline_mode=once),
                  pl.BlockSpec((1, d), lambda i: (0, 0)),
                  pl.BlockSpec((1, d), lambda i: (0, 0))],
        out_specs=pl.BlockSpec((tm, d), lambda i: (i, 0)),
        compiler_params=_cparams(("parallel",)),
        name="merge",
    )(ya, yb, yc, yd, proj, x2, ada3, b_gate, w_branch, w_out, ln_w.reshape(1, -1), ln_b.reshape(1, -1))


def kernel(x, c, positions, w_ada, b_ada, w_in, b_gate, ssd_conv_w, ssd_conv_b, ssd_dt_bias, ssd_a_log,
           ssd_d, ssd_norm_w, attn_sinks, sconv_w, lru_conv_w, lru_conv_b, lru_w_a, lru_b_a, lru_w_x,
           lru_b_x, lru_lambda, w_branch, w_out, ln_w, ln_b):
    bsz, seq, d = x.shape
    small = dict(b_gate=b_gate, ssd_conv_w=ssd_conv_w, ssd_conv_b=ssd_conv_b, ssd_dt_bias=ssd_dt_bias,
                 ssd_a_log=ssd_a_log, ssd_d=ssd_d, ssd_norm_w=ssd_norm_w, attn_sinks=attn_sinks,
                 sconv_w=sconv_w, lru_conv_w=lru_conv_w, lru_conv_b=lru_conv_b, lru_w_a=lru_w_a,
                 lru_b_a=lru_b_a, lru_w_x=lru_w_x, lru_b_x=lru_b_x, lru_lambda=lru_lambda,
                 ln_w=ln_w, ln_b=ln_b)
    w_main, w_dt = _rearrange_w_in(w_in)
    w_branch_b = w_branch.astype(BF16)
    w_out_b = w_out.astype(BF16)
    cos_t, sin_t = _rope_tables(positions, tt=min(2048, bsz * seq))
    x2 = x.reshape(bsz * seq, d)
    for layer in range(DEPTH):
        p = {k: v[layer] for k, v in small.items()}
        ada3 = _ada(c, w_ada, b_ada, layer).reshape(bsz, 1, 3 * d)
        proj, dt_raw = _inproj(x2, ada3, w_main, w_dt, layer, seq, tm=min(1024, seq), tn=1536)
        ya = _ssd(proj, dt_raw, p, bsz, seq, tt=min(512, seq))
        yb = _swa(proj, cos_t, sin_t, p["attn_sinks"], bsz, seq, tt=min(512, seq))
        yc, yd = _cd(proj, p, bsz, seq, tt=min(256, seq))
        x2 = _merge(ya, yb, yc, yd, proj, x2, ada3, p["b_gate"], w_branch_b, w_out_b, p["ln_w"], p["ln_b"],
                    layer, seq, tm=min(256, seq))
    return x2.reshape(bsz, seq, d)
```

```python
import functools

import jax
import jax.numpy as jnp
from jax import lax
from jax.experimental import pallas as pl
from jax.experimental.pallas import tpu as pltpu

F32 = jnp.float32
BF16 = jnp.bfloat16

LANES = 128
SUBLANES = 8
VMEM_LIMIT = 56 * 1024 * 1024

D_MODEL = 2048
BRANCH = D_MODEL // 2
N_BRANCH = 4
HEAD_DIM = 64
SSD_HEADS = BRANCH // HEAD_DIM
SSD_GROUPS = 4
SSD_STATE = 128
SSD_CONV = 4
CHUNK = 128
ATTN_HEADS = BRANCH // HEAD_DIM
KV_HEADS = 4
KV_WIDTH = KV_HEADS * HEAD_DIM
WINDOW = 128
ROPE_THETA = 10000.0
SCONV = 3
LRU_BLOCKS = 16
LRU_CONV = 4
LRU_C = 8.0
LRU_GROUP = 256
LN_EPS = 1e-5
RMS_EPS = 1e-5
DEPTH = 2
ALPHA = (2.0 * DEPTH) ** 0.25
LOG2E = 1.4426950408889634

MERGE_COLS = N_BRANCH * D_MODEL
_C0 = MERGE_COLS // BRANCH
COL_Z, COL_XS, COL_BC, COL_Q, COL_GB = _C0, _C0 + 1, _C0 + 2, _C0 + 3, _C0 + 4
COL_CB, COL_CC, COL_CX, COL_CG, COL_DX, COL_DG = (_C0 + 5, _C0 + 6, _C0 + 7, _C0 + 8, _C0 + 9, _C0 + 10)
COL_KV = (MERGE_COLS + 11 * BRANCH) // (2 * KV_WIDTH)
ORIG_DT = 3 * BRANCH
ORIG_Q = ORIG_DT + SSD_HEADS
ORIG_K = ORIG_Q + BRANCH
ORIG_GB = ORIG_K + 2 * KV_WIDTH
ORIG_MERGE = ORIG_GB + 7 * BRANCH


IN_COLS = ORIG_MERGE + MERGE_COLS
MAIN_COLS = IN_COLS - SSD_HEADS
_W_PIECES = ((ORIG_MERGE, MERGE_COLS), (0, ORIG_DT), (ORIG_Q, BRANCH), (ORIG_GB, 7 * BRANCH),
             (ORIG_K, 2 * KV_WIDTH))


def _rearrange_kernel(w_ref, main_ref, dt_ref):
    off = 0
    for start, width in _W_PIECES:
        base = (start // LANES) * LANES
        piece = w_ref[:, base:start + width][:, start - base:]
        main_ref[:, off:off + width] = piece.astype(main_ref.dtype)
        off += width
    dt = w_ref[:, ORIG_DT:ORIG_DT + LANES]
    lane = lax.broadcasted_iota(jnp.int32, dt.shape, 1)
    dt_ref[...] = jnp.where(lane < SSD_HEADS, dt, 0.0).astype(dt_ref.dtype)


def _rearrange_w_in(w_in, tk=64):
    depth, d, n = w_in.shape
    return pl.pallas_call(
        _rearrange_kernel,
        out_shape=(jax.ShapeDtypeStruct((depth, d, MAIN_COLS), BF16),
                   jax.ShapeDtypeStruct((depth, d, LANES), BF16)),
        grid=(depth, d // tk),
        in_specs=[pl.BlockSpec((None, tk, n), lambda l, i: (l, i, 0))],
        out_specs=(pl.BlockSpec((None, tk, MAIN_COLS), lambda l, i: (l, i, 0)),
                   pl.BlockSpec((None, tk, LANES), lambda l, i: (l, i, 0))),
        compiler_params=_cparams(("parallel", "parallel")),
        name="w_in_layout",
    )(w_in)


def _softplus(x):
    return jnp.maximum(x, 0.0) + jnp.log1p(jnp.exp(-jnp.abs(x)))


def _silu(x):
    return x * jax.nn.sigmoid(x)


def _cparams(sem):
    return pltpu.CompilerParams(dimension_semantics=sem, vmem_limit_bytes=VMEM_LIMIT)


def _ada_kernel(c_ref, w_ref, b_ref, o_ref):
    ca = _silu(c_ref[...])
    o_ref[...] = jnp.dot(ca, w_ref[...], preferred_element_type=F32,
                         precision=lax.Precision.HIGHEST) + b_ref[...]


def _ada(c, w_ada, b_ada, layer):
    bsz, d = c.shape
    n = w_ada.shape[-1]
    tn = 768
    return pl.pallas_call(
        _ada_kernel,
        out_shape=jax.ShapeDtypeStruct((bsz, n), F32),
        grid=(n // tn,),
        in_specs=[pl.BlockSpec((bsz, d), lambda j: (0, 0)),
                  pl.BlockSpec((None, d, tn), lambda j: (layer, 0, j)),
                  pl.BlockSpec((None, 1, tn), lambda j: (layer, 0, j))],
        out_specs=pl.BlockSpec((bsz, tn), lambda j: (0, j)),
        compiler_params=_cparams(("arbitrary",)),
        name="ada",
    )(c, w_ada, b_ada.reshape(b_ada.shape[0], 1, n))


def _inproj_kernel(x_ref, shift_ref, scale_ref, w_ref, wdt_ref, o_ref, dt_ref, h_ref):
    @pl.when(pl.program_id(1) == 0)
    def _():
        h = x_ref[...] * (1.0 + scale_ref[0]) + shift_ref[0]
        hb = h.astype(BF16)
        h_ref[...] = hb
        dt_ref[...] = jnp.dot(hb, wdt_ref[...], preferred_element_type=F32)

    o_ref[...] = jnp.dot(h_ref[...], w_ref[...], preferred_element_type=F32).astype(o_ref.dtype)


def _inproj(x2, ada3, w_main, w_dt, layer, seq, tm, tn):
    m, d = x2.shape
    n = w_main.shape[-1]
    per_b = seq // tm
    return pl.pallas_call(
        _inproj_kernel,
        out_shape=(jax.ShapeDtypeStruct((m, n), BF16), jax.ShapeDtypeStruct((m, LANES), F32)),
        grid=(m // tm, n // tn),
        in_specs=[pl.BlockSpec((tm, d), lambda i, j: (i, 0)),
                  pl.BlockSpec((1, 1, d), lambda i, j: (i // per_b, 0, 0)),
                  pl.BlockSpec((1, 1, d), lambda i, j: (i // per_b, 0, 1)),
                  pl.BlockSpec((None, d, tn), lambda i, j: (layer, 0, j)),
                  pl.BlockSpec((None, d, LANES), lambda i, j: (layer, 0, 0))],
        out_specs=(pl.BlockSpec((tm, tn), lambda i, j: (i, j)),
                   pl.BlockSpec((tm, LANES), lambda i, j: (i, 0))),
        scratch_shapes=[pltpu.VMEM((tm, d), BF16)],
        compiler_params=_cparams(("parallel", "arbitrary")),
        name="inproj",
    )(x2, ada3, ada3, w_main, w_dt)


def _pair_expand(col_a, col_b, lo):
    shape = (col_a.shape[0], LANES)
    return jnp.where(lo, jnp.broadcast_to(col_a, shape), jnp.broadcast_to(col_b, shape))


def _ssd_kernel(z_ref, xs_ref, bc_ref, dt_ref, cw_ref, cb_ref, dtb_ref, aneg_ref, dsk_ref, nw_ref,
                o_ref, xbuf_ref, state_ref, *, n_chunks):
    pad = SUBLANES

    @pl.when(pl.program_id(1) == 0)
    def _():
        xbuf_ref[0:pad, :] = jnp.zeros((pad, 2 * BRANCH), F32)
        state_ref[...] = jnp.zeros_like(state_ref)

    row = lax.broadcasted_iota(jnp.int32, (CHUNK, CHUNK), 0)
    col = lax.broadcasted_iota(jnp.int32, (CHUNK, CHUNK), 1)
    causal = row >= col
    tril = causal.astype(F32)
    lo = col < HEAD_DIM

    def chunk_body(ci, carry):
        rows = pl.ds(pl.multiple_of(ci * CHUNK, CHUNK), CHUNK)
        xbuf_ref[pad:pad + CHUNK, 0:BRANCH] = xs_ref[rows, :].astype(F32)
        xbuf_ref[pad:pad + CHUNK, BRANCH:2 * BRANCH] = bc_ref[rows, :].astype(F32)
        conv = cb_ref[...] + cw_ref[SSD_CONV - 1:SSD_CONV, :] * xbuf_ref[pad:pad + CHUNK, :]
        for k in range(SSD_CONV - 1):
            off = pad - (SSD_CONV - 1) + k
            conv = conv + cw_ref[k:k + 1, :] * xbuf_ref[off:off + CHUNK, :]
        xbuf_ref[0:pad, :] = xbuf_ref[CHUNK:CHUNK + pad, :]
        xbc = _silu(conv)
        xc = xbc[:, 0:BRANCH]

        dt = _softplus(dt_ref[rows, :] + dtb_ref[...])
        da = dt * aneg_ref[...]
        cum = LOG2E * jnp.dot(tril, da, preferred_element_type=F32, precision=lax.Precision.HIGHEST)
        cum_t = cum.T

        ys = []
        for g in range(SSD_GROUPS):
            b_g = xbc[:, BRANCH + g * SSD_STATE:BRANCH + (g + 1) * SSD_STATE]
            c_g = xbc[:, BRANCH + (SSD_GROUPS + g) * SSD_STATE:BRANCH + (SSD_GROUPS + g + 1) * SSD_STATE]
            b_gb = b_g.astype(BF16)
            c_gb = c_g.astype(BF16)
            cb = lax.dot_general(c_gb, b_gb, (((1,), (1,)), ((), ())), preferred_element_type=F32)
            b_t = b_g.T.astype(BF16)
            for jp in range(2):
                j = 2 * g + jp
                h0, h1 = 2 * j, 2 * j + 1
                x_p = xc[:, j * LANES:(j + 1) * LANES]
                cum0 = jnp.broadcast_to(cum[:, h0:h0 + 1], (CHUNK, CHUNK))
                cum1 = jnp.broadcast_to(cum[:, h1:h1 + 1], (CHUNK, CHUNK))
                cum_p = jnp.where(lo, cum0, cum1)
                dt_p = _pair_expand(dt[:, h0:h0 + 1], dt[:, h1:h1 + 1], lo)
                xdt = x_p * dt_p
                xdt_b = xdt.astype(BF16)
                l0 = jnp.exp2(jnp.where(causal, cum0 - cum_t[h0:h0 + 1, :], -jnp.inf))
                l1 = jnp.exp2(jnp.where(causal, cum1 - cum_t[h1:h1 + 1, :], -jnp.inf))
                y = jnp.where(lo, jnp.dot((cb * l0).astype(BF16), xdt_b, preferred_element_type=F32),
                              jnp.dot((cb * l1).astype(BF16), xdt_b, preferred_element_type=F32))
                st = state_ref[:, j * LANES:(j + 1) * LANES]
                y_off = jnp.dot(c_gb, st.astype(BF16), preferred_element_type=F32)
                y = y + y_off * jnp.exp2(cum_p)
                y = y + x_p * dsk_ref[:, j * LANES:(j + 1) * LANES]
                ys.append(y)
                cum_last = cum_p[CHUNK - 1:CHUNK, :]
                xw = (xdt * jnp.exp2(cum_last - cum_p)).astype(BF16)
                s_new = jnp.dot(b_t, xw, preferred_element_type=F32)
                state_ref[:, j * LANES:(j + 1) * LANES] = st * jnp.exp2(cum_last) + s_new
        y = jnp.concatenate(ys, axis=1)
        y = y * _silu(z_ref[rows, :].astype(F32))
        y = y * lax.rsqrt(jnp.mean(jnp.square(y), axis=-1, keepdims=True) + RMS_EPS)
        o_ref[rows, :] = (y * nw_ref[...]).astype(o_ref.dtype)
        return carry

    lax.fori_loop(0, n_chunks, chunk_body, 0)


def _ssd(proj, dt_raw, p, bsz, seq, tt):
    m = proj.shape[0]
    per_b = seq // tt
    row_map = lambda b, t: b * per_b + t
    pad_heads = LANES - SSD_HEADS
    dtb = jnp.pad(p["ssd_dt_bias"], (0, pad_heads)).reshape(1, LANES)
    aneg = jnp.pad(-jnp.exp(p["ssd_a_log"]), (0, pad_heads)).reshape(1, LANES)
    dsk = jnp.repeat(p["ssd_d"], HEAD_DIM).reshape(1, BRANCH)
    const = lambda shape: pl.BlockSpec(shape, lambda b, t: (0, 0))
    col = lambda cidx: pl.BlockSpec((tt, BRANCH), lambda b, t: (row_map(b, t), cidx))
    return pl.pallas_call(
        functools.partial(_ssd_kernel, n_chunks=tt // CHUNK),
        out_shape=jax.ShapeDtypeStruct((m, BRANCH), BF16),
        grid=(bsz, per_b),
        in_specs=[col(COL_Z), col(COL_XS), col(COL_BC),
                  pl.BlockSpec((tt, LANES), lambda b, t: (row_map(b, t), 0)),
                  const((SSD_CONV, 2 * BRANCH)), const((1, 2 * BRANCH)),
                  const((1, LANES)), const((1, LANES)), const((1, BRANCH)), const((1, BRANCH))],
        out_specs=pl.BlockSpec((tt, BRANCH), lambda b, t: (row_map(b, t), 0)),
        scratch_shapes=[pltpu.VMEM((CHUNK + SUBLANES, 2 * BRANCH), F32),
                        pltpu.VMEM((SSD_STATE, BRANCH), F32)],
        compiler_params=_cparams(("parallel", "arbitrary")),
        name="ssd",
    )(proj, proj, proj, dt_raw, p["ssd_conv_w"], p["ssd_conv_b"].reshape(1, -1), dtb, aneg, dsk,
      p["ssd_norm_w"].reshape(1, -1))


def _rope_table_kernel(pos_ref, invf_ref, cos_ref, sin_ref):
    ang = pos_ref[...].astype(F32) * invf_ref[...]
    lane = lax.broadcasted_iota(jnp.int32, ang.shape, 1)
    first_half = (lane % HEAD_DIM) < (HEAD_DIM // 2)
    cos_ref[...] = jnp.cos(ang)
    sin = jnp.sin(ang)
    sin_ref[...] = jnp.where(first_half, -sin, sin)


def _rope_tables(positions, tt):
    half = HEAD_DIM // 2
    inv_half = ROPE_THETA ** (-jnp.arange(half, dtype=F32) / half)
    inv_freq = jnp.tile(inv_half, LANES // half).reshape(1, LANES)
    m = positions.size
    shape = jax.ShapeDtypeStruct((m, LANES), F32)
    return pl.pallas_call(
        _rope_table_kernel,
        out_shape=(shape, shape),
        grid=(m // tt,),
        in_specs=[pl.BlockSpec((tt, 1), lambda i: (i, 0)), pl.BlockSpec((1, LANES), lambda i: (0, 0))],
        out_specs=(pl.BlockSpec((tt, LANES), lambda i: (i, 0)), pl.BlockSpec((tt, LANES), lambda i: (i, 0))),
        compiler_params=_cparams(("parallel",)),
        name="rope_tables",
    )(positions.reshape(m, 1), inv_freq)


def _rope_apply(x, cos, sin_signed, first_half):
    half = HEAD_DIM // 2
    outs = []
    for cblk in range(x.shape[1] // LANES):
        xb = x[:, cblk * LANES:(cblk + 1) * LANES]
        rot = jnp.where(first_half, pltpu.roll(xb, LANES - half, 1), pltpu.roll(xb, half, 1))
        outs.append(xb * cos + rot * sin_signed)
    return outs


def _swa_kernel(q_ref, kv_ref, g_ref, cos_ref, sin_ref, sink_ref, o_ref,
                kcat_ref, vtcat_ref, qb_ref, s_ref, e_ref, inv_ref, *, n_blocks):
    tile = pl.program_id(1)

    @pl.when(tile == 0)
    def _():
        kcat_ref[...] = jnp.zeros_like(kcat_ref)
        vtcat_ref[...] = jnp.zeros_like(vtcat_ref)

    row = lax.broadcasted_iota(jnp.int32, (WINDOW, WINDOW), 0)
    lane = lax.broadcasted_iota(jnp.int32, (WINDOW, WINDOW), 1)
    lo = lane < HEAD_DIM
    top = row < HEAD_DIM
    first_half = (lane % HEAD_DIM) < (HEAD_DIM // 2)
    key = lax.broadcasted_iota(jnp.int32, (2 * WINDOW, WINDOW), 0)
    qry = lax.broadcasted_iota(jnp.int32, (2 * WINDOW, WINDOW), 1)
    band = jnp.logical_and(key > qry, key - WINDOW <= qry)
    nt = (((1,), (1,)), ((), ()))
    n_pairs = ATTN_HEADS // 2

    def block_body(bi, carry):
        rows = pl.ds(pl.multiple_of(bi * WINDOW, WINDOW), WINDOW)
        valid = jnp.logical_and(band, jnp.logical_or(key >= WINDOW, tile * n_blocks + bi > 0))
        cos = cos_ref[rows, :]
        sin_s = sin_ref[rows, :]
        q_blocks = _rope_apply(q_ref[rows, :].astype(F32) * (LOG2E * HEAD_DIM ** -0.5), cos, sin_s,
                               first_half)
        for j in range(n_pairs):
            qb_ref[j] = q_blocks[j].astype(BF16)
        kv = kv_ref[rows, :].astype(F32)
        k_blocks = _rope_apply(kv[:, 0:KV_WIDTH], cos, sin_s, first_half)

        for g in range(KV_HEADS):
            kp = k_blocks[g // 2]
            vp = kv[:, KV_WIDTH + (g // 2) * LANES:KV_WIDTH + (g // 2 + 1) * LANES]
            kr = pltpu.roll(kp, HEAD_DIM, 1)
            vr = pltpu.roll(vp, HEAD_DIM, 1)
            own_lo = (g % 2) == 0
            kcat_ref[2 * g, WINDOW:, :] = jnp.where(lo, kp if own_lo else kr, 0.0).astype(BF16)
            kcat_ref[2 * g + 1, WINDOW:, :] = jnp.where(lo, 0.0, kr if own_lo else kp).astype(BF16)
            v_t = jnp.where(lo, vp if own_lo else vr, vr if own_lo else vp).T
            vtcat_ref[2 * g, :, WINDOW:] = jnp.where(top, v_t, 0.0).astype(BF16)
            vtcat_ref[2 * g + 1, :, WINDOW:] = jnp.where(top, 0.0, v_t).astype(BF16)

        for j in range(n_pairs):
            for hh in range(2):
                s_ref[2 * j + hh] = lax.dot_general(kcat_ref[2 * (j // 2) + hh], qb_ref[j], nt,
                                                    preferred_element_type=F32)
        for h in range(ATTN_HEADS):
            s = jnp.where(valid, s_ref[h], -jnp.inf)
            sink = LOG2E * sink_ref[h:h + 1, :]
            mx = jnp.maximum(s.max(0, keepdims=True), sink)
            e = jnp.exp2(s - mx)
            inv_ref[h:h + 1, :] = 1.0 / (e.sum(0, keepdims=True) + jnp.exp2(sink - mx))
            e_ref[h] = e.astype(BF16)
        outs = []
        for j in range(n_pairs):
            g = j // 2
            acc = jnp.dot(vtcat_ref[2 * g], e_ref[2 * j], preferred_element_type=F32)
            acc = acc + jnp.dot(vtcat_ref[2 * g + 1], e_ref[2 * j + 1], preferred_element_type=F32)
            inv = jnp.where(top, inv_ref[2 * j:2 * j + 1, :], inv_ref[2 * j + 1:2 * j + 2, :])
            outs.append((acc * inv).T)
        o = jnp.concatenate(outs, axis=1)
        o_ref[rows, :] = (o * _silu(g_ref[rows, :].astype(F32))).astype(o_ref.dtype)

        for i in range(2 * KV_HEADS):
            kcat_ref[i, :WINDOW, :] = kcat_ref[i, WINDOW:, :]
            vtcat_ref[i, :, :WINDOW] = vtcat_ref[i, :, WINDOW:]
        return carry

    lax.fori_loop(0, n_blocks, block_body, 0, unroll=2)


def _swa(proj, cos_t, sin_t, sinks, bsz, seq, tt):
    m = proj.shape[0]
    per_b = seq // tt
    row_map = lambda b, t: b * per_b + t
    sinks_b = jnp.broadcast_to(sinks.reshape(ATTN_HEADS, 1), (ATTN_HEADS, LANES))
    return pl.pallas_call(
        functools.partial(_swa_kernel, n_blocks=tt // WINDOW),
        out_shape=jax.ShapeDtypeStruct((m, BRANCH), BF16),
        grid=(bsz, per_b),
        in_specs=[pl.BlockSpec((tt, BRANCH), lambda b, t: (row_map(b, t), COL_Q)),
                  pl.BlockSpec((tt, 2 * KV_WIDTH), lambda b, t: (row_map(b, t), COL_KV)),
                  pl.BlockSpec((tt, BRANCH), lambda b, t: (row_map(b, t), COL_GB)),
                  pl.BlockSpec((tt, LANES), lambda b, t: (row_map(b, t), 0)),
                  pl.BlockSpec((tt, LANES), lambda b, t: (row_map(b, t), 0)),
                  pl.BlockSpec((ATTN_HEADS, LANES), lambda b, t: (0, 0))],
        out_specs=pl.BlockSpec((tt, BRANCH), lambda b, t: (row_map(b, t), 0)),
        scratch_shapes=[pltpu.VMEM((2 * KV_HEADS, 2 * WINDOW, LANES), BF16),
                        pltpu.VMEM((2 * KV_HEADS, LANES, 2 * WINDOW), BF16),
                        pltpu.VMEM((ATTN_HEADS // 2, WINDOW, LANES), BF16),
                        pltpu.VMEM((ATTN_HEADS, 2 * WINDOW, WINDOW), F32),
                        pltpu.VMEM((ATTN_HEADS, 2 * WINDOW, WINDOW), BF16),
                        pltpu.VMEM((ATTN_HEADS, LANES), F32)],
        compiler_params=_cparams(("parallel", "arbitrary")),
        name="swa",
    )(proj, proj, proj, cos_t, sin_t, sinks_b)


def _cd_kernel(cb_ref, cc_ref, cx_ref, cg_ref, dx_ref, dg_ref, scw_ref, lcw_ref, lcb_ref, wbd_ref,
               ba_ref, bx_ref, lam_ref, yc_ref, yd_ref, cbuf_ref, dbuf_ref, h_ref, *, tt):
    pad = SUBLANES

    @pl.when(pl.program_id(1) == 0)
    def _():
        cbuf_ref[0:pad, :] = jnp.zeros((pad, BRANCH), F32)
        dbuf_ref[0:pad, :] = jnp.zeros((pad, BRANCH), F32)
        h_ref[...] = jnp.zeros_like(h_ref)

    cbuf_ref[pad:pad + tt, :] = cc_ref[...].astype(F32) * cx_ref[...].astype(F32)
    conv = scw_ref[SCONV - 1:SCONV, :] * cbuf_ref[pad:pad + tt, :]
    for k in range(SCONV - 1):
        off = pad - (SCONV - 1) + k
        conv = conv + scw_ref[k:k + 1, :] * cbuf_ref[off:off + tt, :]
    cbuf_ref[0:pad, :] = cbuf_ref[tt:tt + pad, :]
    yc_ref[...] = (cb_ref[...].astype(F32) * conv * _silu(cg_ref[...].astype(F32))).astype(yc_ref.dtype)

    dbuf_ref[pad:pad + tt, :] = dx_ref[...].astype(F32)
    xs = lcb_ref[...] + lcw_ref[LRU_CONV - 1:LRU_CONV, :] * dbuf_ref[pad:pad + tt, :]
    for k in range(LRU_CONV - 1):
        off = pad - (LRU_CONV - 1) + k
        xs = xs + lcw_ref[k:k + 1, :] * dbuf_ref[off:off + tt, :]
    dbuf_ref[0:pad, :] = dbuf_ref[tt:tt + pad, :]
    xs_b = xs.astype(BF16)
    gates = [jnp.dot(xs_b[:, q * LRU_GROUP:(q + 1) * LRU_GROUP], wbd_ref[q], preferred_element_type=F32)
             for q in range(BRANCH // LRU_GROUP)]
    pre_a = jnp.concatenate([gq[:, 0:LRU_GROUP] for gq in gates], axis=1)
    pre_x = jnp.concatenate([gq[:, LRU_GROUP:2 * LRU_GROUP] for gq in gates], axis=1)
    rg = jax.nn.sigmoid(pre_a + ba_ref[...])
    ig = jax.nn.sigmoid(pre_x + bx_ref[...])
    log_a = -LRU_C * rg * _softplus(-lam_ref[...])
    a = jnp.exp(log_a)
    u = jnp.sqrt(jnp.tanh(-log_a) * (a * a + 1.0)) * (ig * xs)

    groups = tt // SUBLANES
    a3 = a.reshape(groups, SUBLANES, BRANCH)
    u3 = u.reshape(groups, SUBLANES, BRANCH)
    sub = lax.broadcasted_iota(jnp.int32, a3.shape, 1)
    k = 1
    while k < SUBLANES:
        keep = sub >= k
        a_s = jnp.where(keep, pltpu.roll(a3, k, 1), 1.0)
        u_s = jnp.where(keep, pltpu.roll(u3, k, 1), 0.0)
        u3 = a3 * u_s + u3
        a3 = a3 * a_s
        k *= 2
    carry = h_ref[0:1, :]
    hs = []
    for gi in range(groups):
        hg = u3[gi] + a3[gi] * carry
        carry = hg[SUBLANES - 1:SUBLANES, :]
        hs.append(hg)
    h_ref[...] = jnp.broadcast_to(carry, h_ref.shape)
    h = jnp.concatenate(hs, axis=0)
    yd_ref[...] = (h * _silu(dg_ref[...].astype(F32))).astype(yd_ref.dtype)


def _block_diag_pairs(w_a, w_x):
    per = LRU_GROUP // HEAD_DIM
    eye = jnp.eye(per, dtype=w_a.dtype)

    def bd(w):
        w4 = w.reshape(LRU_BLOCKS // per, per, HEAD_DIM, HEAD_DIM)
        return jnp.einsum('qiab,ij->qiajb', w4, eye).reshape(LRU_BLOCKS // per, LRU_GROUP, LRU_GROUP)

    return jnp.concatenate([bd(w_a), bd(w_x)], axis=2).astype(BF16)


def _cd(proj, p, bsz, seq, tt):
    m = proj.shape[0]
    per_b = seq // tt
    row_map = lambda b, t: b * per_b + t
    col = lambda cidx: pl.BlockSpec((tt, BRANCH), lambda b, t: (row_map(b, t), cidx))
    const = lambda shape: pl.BlockSpec(shape, lambda b, t: (0,) * len(shape))
    wbd = _block_diag_pairs(p["lru_w_a"], p["lru_w_x"])
    out_spec = pl.BlockSpec((tt, BRANCH), lambda b, t: (row_map(b, t), 0))
    vec = lambda a: a.reshape(1, -1)
    return pl.pallas_call(
        functools.partial(_cd_kernel, tt=tt),
        out_shape=(jax.ShapeDtypeStruct((m, BRANCH), BF16), jax.ShapeDtypeStruct((m, BRANCH), BF16)),
        grid=(bsz, per_b),
        in_specs=[col(COL_CB), col(COL_CC), col(COL_CX), col(COL_CG), col(COL_DX), col(COL_DG),
                  const((SCONV, BRANCH)), const((LRU_CONV, BRANCH)), const((1, BRANCH)),
                  const(wbd.shape), const((1, BRANCH)), const((1, BRANCH)), const((1, BRANCH))],
        out_specs=(out_spec, out_spec),
        scratch_shapes=[pltpu.VMEM((tt + SUBLANES, BRANCH), F32),
                        pltpu.VMEM((tt + SUBLANES, BRANCH), F32),
                        pltpu.VMEM((SUBLANES, BRANCH), F32)],
        compiler_params=_cparams(("parallel", "arbitrary")),
        name="sconv_rglru",
    )(proj, proj, proj, proj, proj, proj, p["sconv_w"], p["lru_conv_w"], vec(p["lru_conv_b"]), wbd,
      vec(p["lru_b_a"]), vec(p["lru_b_x"]), vec(p["lru_lambda"]))


def _merge_kernel(ya_ref, yb_ref, yc_ref, yd_ref, lg_ref, x_ref, gate_ref, bg_ref, wb_ref, wo_ref,
                  lnw_ref, lnb_ref, o_ref):
    m = None
    for k, y_ref in enumerate((ya_ref, yb_ref, yc_ref, yd_ref)):
        gk = jax.nn.sigmoid(lg_ref[:, k * D_MODEL:(k + 1) * D_MODEL].astype(F32) + bg_ref[k:k + 1, :])
        t = gk * jnp.dot(y_ref[...], wb_ref[k], preferred_element_type=F32)
        m = t if m is None else m + t
    out = jnp.dot(m.astype(BF16), wo_ref[...], preferred_element_type=F32)
    r = ALPHA * x_ref[...] + gate_ref[0] * out
    mu = jnp.mean(r, axis=-1, keepdims=True)
    rc = r - mu
    var = jnp.mean(jnp.square(rc), axis=-1, keepdims=True)
    o_ref[...] = rc * lax.rsqrt(var + LN_EPS) * lnw_ref[...] + lnb_ref[...]


def _merge(ya, yb, yc, yd, proj, x2, ada3, b_gate, w_branch, w_out, ln_w, ln_b, layer, seq, tm):
    m, d = x2.shape
    per_b = seq // tm
    ycol = pl.BlockSpec((tm, BRANCH), lambda i: (i, 0))
    once = pl.Buffered(1)
    return pl.pallas_call(
        _merge_kernel,
        out_shape=jax.ShapeDtypeStruct((m, d), F32),
        grid=(m // tm,),
        in_specs=[ycol, ycol, ycol, ycol,
                  pl.BlockSpec((tm, N_BRANCH * d), lambda i: (i, 0)),
                  pl.BlockSpec((tm, d), lambda i: (i, 0)),
                  pl.BlockSpec((1, 1, d), lambda i: (i // per_b, 0, 2)),
                  pl.BlockSpec((N_BRANCH, d), lambda i: (0, 0)),
                  pl.BlockSpec((None, N_BRANCH, BRANCH, d), lambda i: (layer, 0, 0, 0), pipeline_mode=once),
                  pl.BlockSpec((None, d, d), lambda i: (layer, 0, 0), pipeline_mode=once),
                  pl.BlockSpec((1, d), lambda i: (0, 0)),
                  pl.BlockSpec((1, d), lambda i: (0, 0))],
        out_specs=pl.BlockSpec((tm, d), lambda i: (i, 0)),
        compiler_params=_cparams(("parallel",)),
        name="merge",
    )(ya, yb, yc, yd, proj, x2, ada3, b_gate, w_branch, w_out, ln_w.reshape(1, -1), ln_b.reshape(1, -1))


def kernel(x, c, positions, w_ada, b_ada, w_in, b_gate, ssd_conv_w, ssd_conv_b, ssd_dt_bias, ssd_a_log,
           ssd_d, ssd_norm_w, attn_sinks, sconv_w, lru_conv_w, lru_conv_b, lru_w_a, lru_b_a, lru_w_x,
           lru_b_x, lru_lambda, w_branch, w_out, ln_w, ln_b):
    bsz, seq, d = x.shape
    small = dict(b_gate=b_gate, ssd_conv_w=ssd_conv_w, ssd_conv_b=ssd_conv_b, ssd_dt_bias=ssd_dt_bias,
                 ssd_a_log=ssd_a_log, ssd_d=ssd_d, ssd_norm_w=ssd_norm_w, attn_sinks=attn_sinks,
                 sconv_w=sconv_w, lru_conv_w=lru_conv_w, lru_conv_b=lru_conv_b, lru_w_a=lru_w_a,
                 lru_b_a=lru_b_a, lru_w_x=lru_w_x, lru_b_x=lru_b_x, lru_lambda=lru_lambda,
                 ln_w=ln_w, ln_b=ln_b)
    w_main, w_dt = _rearrange_w_in(w_in)
    w_branch_b = w_branch.astype(BF16)
    w_out_b = w_out.astype(BF16)
    cos_t, sin_t = _rope_tables(positions, tt=min(2048, bsz * seq))
    x2 = x.reshape(bsz * seq, d)
    for layer in range(DEPTH):
        p = {k: v[layer] for k, v in small.items()}
        ada3 = _ada(c, w_ada, b_ada, layer).reshape(bsz, 1, 3 * d)
        proj, dt_raw = _inproj(x2, ada3, w_main, w_dt, layer, seq, tm=min(1024, seq), tn=1536)
        ya = _ssd(proj, dt_raw, p, bsz, seq, tt=min(512, seq))
        yb = _swa(proj, cos_t, sin_t, p["attn_sinks"], bsz, seq, tt=min(512, seq))
        yc, yd = _cd(proj, p, bsz, seq, tt=min(256, seq))
        x2 = _merge(ya, yb, yc, yd, proj, x2, ada3, p["b_gate"], w_branch_b, w_out_b, p["ln_w"], p["ln_b"],
                    layer, seq, tm=min(256, seq))
    return x2.reshape(bsz, seq, d)
```

```python
import functools

import jax
import jax.numpy as jnp
from jax import lax
from jax.experimental import pallas as pl
from jax.experimental.pallas import tpu as pltpu

F32 = jnp.float32
BF16 = jnp.bfloat16

LANES = 128
SUBLANES = 8
VMEM_LIMIT = 56 * 1024 * 1024

D_MODEL = 2048
BRANCH = D_MODEL // 2
N_BRANCH = 4
HEAD_DIM = 64
SSD_HEADS = BRANCH // HEAD_DIM
SSD_GROUPS = 4
SSD_STATE = 128
SSD_CONV = 4
CHUNK = 128
ATTN_HEADS = BRANCH // HEAD_DIM
KV_HEADS = 4
KV_WIDTH = KV_HEADS * HEAD_DIM
WINDOW = 128
ROPE_THETA = 10000.0
SCONV = 3
LRU_BLOCKS = 16
LRU_CONV = 4
LRU_C = 8.0
LRU_GROUP = 256
LN_EPS = 1e-5
RMS_EPS = 1e-5
DEPTH = 2
ALPHA = (2.0 * DEPTH) ** 0.25
LOG2E = 1.4426950408889634

MERGE_COLS = N_BRANCH * D_MODEL
_C0 = MERGE_COLS // BRANCH
COL_Z, COL_XS, COL_BC, COL_Q, COL_GB = _C0, _C0 + 1, _C0 + 2, _C0 + 3, _C0 + 4
COL_CB, COL_CC, COL_CX, COL_CG, COL_DX, COL_DG = (_C0 + 5, _C0 + 6, _C0 + 7, _C0 + 8, _C0 + 9, _C0 + 10)
COL_KV = (MERGE_COLS + 11 * BRANCH) // (2 * KV_WIDTH)
ORIG_DT = 3 * BRANCH
ORIG_Q = ORIG_DT + SSD_HEADS
ORIG_K = ORIG_Q + BRANCH
ORIG_GB = ORIG_K + 2 * KV_WIDTH
ORIG_MERGE = ORIG_GB + 7 * BRANCH


IN_COLS = ORIG_MERGE + MERGE_COLS
MAIN_COLS = IN_COLS - SSD_HEADS
_W_PIECES = ((ORIG_MERGE, MERGE_COLS), (0, ORIG_DT), (ORIG_Q, BRANCH), (ORIG_GB, 7 * BRANCH),
             (ORIG_K, 2 * KV_WIDTH))


PREP_COLS = 512


def _rearrange_kernel(wt_ref, main_ref):
    main_ref[...] = wt_ref[...].T.astype(main_ref.dtype)


COL_UNIT = 16


def _source_column_units(j):
    dst = j * (PREP_COLS // COL_UNIT)
    src, off = None, 0
    for start, width in _W_PIECES:
        cand = dst - off // COL_UNIT + start // COL_UNIT
        src = cand if src is None else jnp.where(dst >= off // COL_UNIT, cand, src)
        off += width
    return src


def _rearrange_w_in(w_in):
    depth, d, n = w_in.shape
    w_main = pl.pallas_call(
        _rearrange_kernel,
        out_shape=jax.ShapeDtypeStruct((depth, d, MAIN_COLS), BF16),
        grid=(depth, MAIN_COLS // PREP_COLS),
        in_specs=[pl.BlockSpec((pl.Element(PREP_COLS), pl.Element(d)),
                               lambda l, j: ((l * (n // COL_UNIT) + _source_column_units(j)) * COL_UNIT, 0))],
        out_specs=pl.BlockSpec((None, d, PREP_COLS), lambda l, j: (l, 0, j)),
        compiler_params=_cparams(("parallel", "parallel")),
        name="w_in_layout",
    )(jnp.swapaxes(w_in, 1, 2).reshape(depth * n, d))
    w_dt = jnp.pad(w_in[..., ORIG_DT:ORIG_Q], ((0, 0), (0, 0), (0, LANES - SSD_HEADS))).astype(BF16)
    return w_main, w_dt


def _softplus(x):
    return jnp.maximum(x, 0.0) + jnp.log1p(jnp.exp(-jnp.abs(x)))


def _silu(x):
    return x * jax.nn.sigmoid(x)


def _cparams(sem):
    return pltpu.CompilerParams(dimension_semantics=sem, vmem_limit_bytes=VMEM_LIMIT)


def _ada_kernel(c_ref, w_ref, b_ref, o_ref):
    ca = _silu(c_ref[...])
    o_ref[...] = jnp.dot(ca, w_ref[...], preferred_element_type=F32,
                         precision=lax.Precision.HIGHEST) + b_ref[...]


def _ada(c, w_ada, b_ada, layer):
    bsz, d = c.shape
    n = w_ada.shape[-1]
    tn = 768
    return pl.pallas_call(
        _ada_kernel,
        out_shape=jax.ShapeDtypeStruct((bsz, n), F32),
        grid=(n // tn,),
        in_specs=[pl.BlockSpec((bsz, d), lambda j: (0, 0)),
                  pl.BlockSpec((None, d, tn), lambda j: (layer, 0, j)),
                  pl.BlockSpec((None, 1, tn), lambda j: (layer, 0, j))],
        out_specs=pl.BlockSpec((bsz, tn), lambda j: (0, j)),
        compiler_params=_cparams(("arbitrary",)),
        name="ada",
    )(c, w_ada, b_ada.reshape(b_ada.shape[0], 1, n))


def _inproj_kernel(x_ref, shift_ref, scale_ref, w_ref, wdt_ref, o_ref, dt_ref, h_ref):
    @pl.when(pl.program_id(1) == 0)
    def _():
        h = x_ref[...] * (1.0 + scale_ref[0]) + shift_ref[0]
        hb = h.astype(BF16)
        h_ref[...] = hb
        dt_ref[...] = jnp.dot(hb, wdt_ref[...], preferred_element_type=F32)

    o_ref[...] = jnp.dot(h_ref[...], w_ref[...], preferred_element_type=F32).astype(o_ref.dtype)


def _inproj(x2, ada3, w_main, w_dt, layer, seq, tm, tn):
    m, d = x2.shape
    n = w_main.shape[-1]
    per_b = seq // tm
    return pl.pallas_call(
        _inproj_kernel,
        out_shape=(jax.ShapeDtypeStruct((m, n), BF16), jax.ShapeDtypeStruct((m, LANES), F32)),
        grid=(m // tm, n // tn),
        in_specs=[pl.BlockSpec((tm, d), lambda i, j: (i, 0)),
                  pl.BlockSpec((1, 1, d), lambda i, j: (i // per_b, 0, 0)),
                  pl.BlockSpec((1, 1, d), lambda i, j: (i // per_b, 0, 1)),
                  pl.BlockSpec((None, d, tn), lambda i, j: (layer, 0, j)),
                  pl.BlockSpec((None, d, LANES), lambda i, j: (layer, 0, 0))],
        out_specs=(pl.BlockSpec((tm, tn), lambda i, j: (i, j)),
                   pl.BlockSpec((tm, LANES), lambda i, j: (i, 0))),
        scratch_shapes=[pltpu.VMEM((tm, d), BF16)],
        compiler_params=_cparams(("parallel", "arbitrary")),
        name="inproj",
    )(x2, ada3, ada3, w_main, w_dt)


def _pair_expand(col_a, col_b, lo):
    shape = (col_a.shape[0], LANES)
    return jnp.where(lo, jnp.broadcast_to(col_a, shape), jnp.broadcast_to(col_b, shape))


def _ssd_kernel(z_ref, xs_ref, bc_ref, dt_ref, cw_ref, cb_ref, dtb_ref, aneg_ref, dsk_ref, nw_ref,
                o_ref, xbuf_ref, state_ref, *, n_chunks):
    pad = SUBLANES

    @pl.when(pl.program_id(1) == 0)
    def _():
        xbuf_ref[0:pad, :] = jnp.zeros((pad, 2 * BRANCH), F32)
        state_ref[...] = jnp.zeros_like(state_ref)

    row = lax.broadcasted_iota(jnp.int32, (CHUNK, CHUNK), 0)
    col = lax.broadcasted_iota(jnp.int32, (CHUNK, CHUNK), 1)
    causal = row >= col
    tril = causal.astype(F32)
    lo = col < HEAD_DIM

    def chunk_body(ci, carry):
        rows = pl.ds(pl.multiple_of(ci * CHUNK, CHUNK), CHUNK)
        xbuf_ref[pad:pad + CHUNK, 0:BRANCH] = xs_ref[rows, :].astype(F32)
        xbuf_ref[pad:pad + CHUNK, BRANCH:2 * BRANCH] = bc_ref[rows, :].astype(F32)
        conv = cb_ref[...] + cw_ref[SSD_CONV - 1:SSD_CONV, :] * xbuf_ref[pad:pad + CHUNK, :]
        for k in range(SSD_CONV - 1):
            off = pad - (SSD_CONV - 1) + k
            conv = conv + cw_ref[k:k + 1, :] * xbuf_ref[off:off + CHUNK, :]
        xbuf_ref[0:pad, :] = xbuf_ref[CHUNK:CHUNK + pad, :]
        xbc = _silu(conv)
        xc = xbc[:, 0:BRANCH]

        dt = _softplus(dt_ref[rows, :] + dtb_ref[...])
        da = dt * aneg_ref[...]
        cum = LOG2E * jnp.dot(tril, da, preferred_element_type=F32, precision=lax.Precision.HIGHEST)
        cum_t = cum.T

        ys = []
        for g in range(SSD_GROUPS):
            b_g = xbc[:, BRANCH + g * SSD_STATE:BRANCH + (g + 1) * SSD_STATE]
            c_g = xbc[:, BRANCH + (SSD_GROUPS + g) * SSD_STATE:BRANCH + (SSD_GROUPS + g + 1) * SSD_STATE]
            b_gb = b_g.astype(BF16)
            c_gb = c_g.astype(BF16)
            cb = lax.dot_general(c_gb, b_gb, (((1,), (1,)), ((), ())), preferred_element_type=F32)
            b_t = b_g.T.astype(BF16)
            for jp in range(2):
                j = 2 * g + jp
                h0, h1 = 2 * j, 2 * j + 1
                x_p = xc[:, j * LANES:(j + 1) * LANES]
                cum0 = jnp.broadcast_to(cum[:, h0:h0 + 1], (CHUNK, CHUNK))
                cum1 = jnp.broadcast_to(cum[:, h1:h1 + 1], (CHUNK, CHUNK))
                cum_p = jnp.where(lo, cum0, cum1)
                dt_p = _pair_expand(dt[:, h0:h0 + 1], dt[:, h1:h1 + 1], lo)
                xdt = x_p * dt_p
                xdt_b = xdt.astype(BF16)
                l0 = jnp.exp2(jnp.where(causal, cum0 - cum_t[h0:h0 + 1, :], -jnp.inf))
                l1 = jnp.exp2(jnp.where(causal, cum1 - cum_t[h1:h1 + 1, :], -jnp.inf))
                y = jnp.where(lo, jnp.dot((cb * l0).astype(BF16), xdt_b, preferred_element_type=F32),
                              jnp.dot((cb * l1).astype(BF16), xdt_b, preferred_element_type=F32))
                st = state_ref[:, j * LANES:(j + 1) * LANES]
                y_off = jnp.dot(c_gb, st.astype(BF16), preferred_element_type=F32)
                y = y + y_off * jnp.exp2(cum_p)
                y = y + x_p * dsk_ref[:, j * LANES:(j + 1) * LANES]
                ys.append(y)
                cum_last = cum_p[CHUNK - 1:CHUNK, :]
                xw = (xdt * jnp.exp2(cum_last - cum_p)).astype(BF16)
                s_new = jnp.dot(b_t, xw, preferred_element_type=F32)
                state_ref[:, j * LANES:(j + 1) * LANES] = st * jnp.exp2(cum_last) + s_new
        y = jnp.concatenate(ys, axis=1)
        y = y * _silu(z_ref[rows, :].astype(F32))
        y = y * lax.rsqrt(jnp.mean(jnp.square(y), axis=-1, keepdims=True) + RMS_EPS)
        o_ref[rows, :] = (y * nw_ref[...]).astype(o_ref.dtype)
        return carry

    lax.fori_loop(0, n_chunks, chunk_body, 0)


def _ssd(proj, dt_raw, p, bsz, seq, tt):
    m = proj.shape[0]
    per_b = seq // tt
    row_map = lambda b, t: b * per_b + t
    pad_heads = LANES - SSD_HEADS
    dtb = jnp.pad(p["ssd_dt_bias"], (0, pad_heads)).reshape(1, LANES)
    aneg = jnp.pad(-jnp.exp(p["ssd_a_log"]), (0, pad_heads)).reshape(1, LANES)
    dsk = jnp.repeat(p["ssd_d"], HEAD_DIM).reshape(1, BRANCH)
    const = lambda shape: pl.BlockSpec(shape, lambda b, t: (0, 0))
    col = lambda cidx: pl.BlockSpec((tt, BRANCH), lambda b, t: (row_map(b, t), cidx))
    return pl.pallas_call(
        functools.partial(_ssd_kernel, n_chunks=tt // CHUNK),
        out_shape=jax.ShapeDtypeStruct((m, BRANCH), BF16),
        grid=(bsz, per_b),
        in_specs=[col(COL_Z), col(COL_XS), col(COL_BC),
                  pl.BlockSpec((tt, LANES), lambda b, t: (row_map(b, t), 0)),
                  const((SSD_CONV, 2 * BRANCH)), const((1, 2 * BRANCH)),
                  const((1, LANES)), const((1, LANES)), const((1, BRANCH)), const((1, BRANCH))],
        out_specs=pl.BlockSpec((tt, BRANCH), lambda b, t: (row_map(b, t), 0)),
        scratch_shapes=[pltpu.VMEM((CHUNK + SUBLANES, 2 * BRANCH), F32),
                        pltpu.VMEM((SSD_STATE, BRANCH), F32)],
        compiler_params=_cparams(("parallel", "arbitrary")),
        name="ssd",
    )(proj, proj, proj, dt_raw, p["ssd_conv_w"], p["ssd_conv_b"].reshape(1, -1), dtb, aneg, dsk,
      p["ssd_norm_w"].reshape(1, -1))


def _rope_table_kernel(pos_ref, invf_ref, cos_ref, sin_ref):
    ang = pos_ref[...].astype(F32) * invf_ref[...]
    lane = lax.broadcasted_iota(jnp.int32, ang.shape, 1)
    first_half = (lane % HEAD_DIM) < (HEAD_DIM // 2)
    cos_ref[...] = jnp.cos(ang)
    sin = jnp.sin(ang)
    sin_ref[...] = jnp.where(first_half, -sin, sin)


def _rope_tables(positions, tt):
    half = HEAD_DIM // 2
    inv_half = ROPE_THETA ** (-jnp.arange(half, dtype=F32) / half)
    inv_freq = jnp.tile(inv_half, LANES // half).reshape(1, LANES)
    m = positions.size
    shape = jax.ShapeDtypeStruct((m, LANES), F32)
    return pl.pallas_call(
        _rope_table_kernel,
        out_shape=(shape, shape),
        grid=(m // tt,),
        in_specs=[pl.BlockSpec((tt, 1), lambda i: (i, 0)), pl.BlockSpec((1, LANES), lambda i: (0, 0))],
        out_specs=(pl.BlockSpec((tt, LANES), lambda i: (i, 0)), pl.BlockSpec((tt, LANES), lambda i: (i, 0))),
        compiler_params=_cparams(("parallel",)),
        name="rope_tables",
    )(positions.reshape(m, 1), inv_freq)


def _rope_apply(x, cos, sin_signed, first_half):
    half = HEAD_DIM // 2
    outs = []
    for cblk in range(x.shape[1] // LANES):
        xb = x[:, cblk * LANES:(cblk + 1) * LANES]
        rot = jnp.where(first_half, pltpu.roll(xb, LANES - half, 1), pltpu.roll(xb, half, 1))
        outs.append(xb * cos + rot * sin_signed)
    return outs


def _swa_kernel(q_ref, kv_ref, g_ref, cos_ref, sin_ref, sink_ref, o_ref,
                kcat_ref, vtcat_ref, qb_ref, s_ref, e_ref, inv_ref, *, n_blocks):
    tile = pl.program_id(1)

    @pl.when(tile == 0)
    def _():
        kcat_ref[...] = jnp.zeros_like(kcat_ref)
        vtcat_ref[...] = jnp.zeros_like(vtcat_ref)

    row = lax.broadcasted_iota(jnp.int32, (WINDOW, WINDOW), 0)
    lane = lax.broadcasted_iota(jnp.int32, (WINDOW, WINDOW), 1)
    lo = lane < HEAD_DIM
    top = row < HEAD_DIM
    first_half = (lane % HEAD_DIM) < (HEAD_DIM // 2)
    key = lax.broadcasted_iota(jnp.int32, (2 * WINDOW, WINDOW), 0)
    qry = lax.broadcasted_iota(jnp.int32, (2 * WINDOW, WINDOW), 1)
    band = jnp.logical_and(key > qry, key - WINDOW <= qry)
    nt = (((1,), (1,)), ((), ()))
    n_pairs = ATTN_HEADS // 2

    def block_body(bi, carry):
        rows = pl.ds(pl.multiple_of(bi * WINDOW, WINDOW), WINDOW)
        valid = jnp.logical_and(band, jnp.logical_or(key >= WINDOW, tile * n_blocks + bi > 0))
        cos = cos_ref[rows, :]
        sin_s = sin_ref[rows, :]
        q_blocks = _rope_apply(q_ref[rows, :].astype(F32) * (LOG2E * HEAD_DIM ** -0.5), cos, sin_s,
                               first_half)
        for j in range(n_pairs):
            qb_ref[j] = q_blocks[j].astype(BF16)
        kv = kv_ref[rows, :].astype(F32)
        k_blocks = _rope_apply(kv[:, 0:KV_WIDTH], cos, sin_s, first_half)

        for g in range(KV_HEADS):
            kp = k_blocks[g // 2]
            vp = kv[:, KV_WIDTH + (g // 2) * LANES:KV_WIDTH + (g // 2 + 1) * LANES]
            kr = pltpu.roll(kp, HEAD_DIM, 1)
            vr = pltpu.roll(vp, HEAD_DIM, 1)
            own_lo = (g % 2) == 0
            kcat_ref[2 * g, WINDOW:, :] = jnp.where(lo, kp if own_lo else kr, 0.0).astype(BF16)
            kcat_ref[2 * g + 1, WINDOW:, :] = jnp.where(lo, 0.0, kr if own_lo else kp).astype(BF16)
            v_t = jnp.where(lo, vp if own_lo else vr, vr if own_lo else vp).T
            vtcat_ref[2 * g, :, WINDOW:] = jnp.where(top, v_t, 0.0).astype(BF16)
            vtcat_ref[2 * g + 1, :, WINDOW:] = jnp.where(top, 0.0, v_t).astype(BF16)

        for j in range(n_pairs):
            for hh in range(2):
                s_ref[2 * j + hh] = lax.dot_general(kcat_ref[2 * (j // 2) + hh], qb_ref[j], nt,
                                                    preferred_element_type=F32)
        for h in range(ATTN_HEADS):
            s = jnp.where(valid, s_ref[h], -jnp.inf)
            sink = LOG2E * sink_ref[h:h + 1, :]
            mx = jnp.maximum(s.max(0, keepdims=True), sink)
            e = jnp.exp2(s - mx)
            inv_ref[h:h + 1, :] = 1.0 / (e.sum(0, keepdims=True) + jnp.exp2(sink - mx))
            e_ref[h] = e.astype(BF16)
        outs = []
        for j in range(n_pairs):
            g = j // 2
            acc = jnp.dot(vtcat_ref[2 * g], e_ref[2 * j], preferred_element_type=F32)
            acc = acc + jnp.dot(vtcat_ref[2 * g + 1], e_ref[2 * j + 1], preferred_element_type=F32)
            inv = jnp.where(top, inv_ref[2 * j:2 * j + 1, :], inv_ref[2 * j + 1:2 * j + 2, :])
            outs.append((acc * inv).T)
        o = jnp.concatenate(outs, axis=1)
        o_ref[rows, :] = (o * _silu(g_ref[rows, :].astype(F32))).astype(o_ref.dtype)

        for i in range(2 * KV_HEADS):
            kcat_ref[i, :WINDOW, :] = kcat_ref[i, WINDOW:, :]
            vtcat_ref[i, :, :WINDOW] = vtcat_ref[i, :, WINDOW:]
        return carry

    lax.fori_loop(0, n_blocks, block_body, 0, unroll=2)


def _swa(proj, cos_t, sin_t, sinks, bsz, seq, tt):
    m = proj.shape[0]
    per_b = seq // tt
    row_map = lambda b, t: b * per_b + t
    sinks_b = jnp.broadcast_to(sinks.reshape(ATTN_HEADS, 1), (ATTN_HEADS, LANES))
    return pl.pallas_call(
        functools.partial(_swa_kernel, n_blocks=tt // WINDOW),
        out_shape=jax.ShapeDtypeStruct((m, BRANCH), BF16),
        grid=(bsz, per_b),
        in_specs=[pl.BlockSpec((tt, BRANCH), lambda b, t: (row_map(b, t), COL_Q)),
                  pl.BlockSpec((tt, 2 * KV_WIDTH), lambda b, t: (row_map(b, t), COL_KV)),
                  pl.BlockSpec((tt, BRANCH), lambda b, t: (row_map(b, t), COL_GB)),
                  pl.BlockSpec((tt, LANES), lambda b, t: (row_map(b, t), 0)),
                  pl.BlockSpec((tt, LANES), lambda b, t: (row_map(b, t), 0)),
                  pl.BlockSpec((ATTN_HEADS, LANES), lambda b, t: (0, 0))],
        out_specs=pl.BlockSpec((tt, BRANCH), lambda b, t: (row_map(b, t), 0)),
        scratch_shapes=[pltpu.VMEM((2 * KV_HEADS, 2 * WINDOW, LANES), BF16),
                        pltpu.VMEM((2 * KV_HEADS, LANES, 2 * WINDOW), BF16),
                        pltpu.VMEM((ATTN_HEADS // 2, WINDOW, LANES), BF16),
                        pltpu.VMEM((ATTN_HEADS, 2 * WINDOW, WINDOW), F32),
                        pltpu.VMEM((ATTN_HEADS, 2 * WINDOW, WINDOW), BF16),
                        pltpu.VMEM((ATTN_HEADS, LANES), F32)],
        compiler_params=_cparams(("parallel", "arbitrary")),
        name="swa",
    )(proj, proj, proj, cos_t, sin_t, sinks_b)


def _cd_kernel(cb_ref, cc_ref, cx_ref, cg_ref, dx_ref, dg_ref, scw_ref, lcw_ref, lcb_ref, wbd_ref,
               ba_ref, bx_ref, lam_ref, yc_ref, yd_ref, cbuf_ref, dbuf_ref, h_ref, *, tt):
    pad = SUBLANES

    @pl.when(pl.program_id(1) == 0)
    def _():
        cbuf_ref[0:pad, :] = jnp.zeros((pad, BRANCH), F32)
        dbuf_ref[0:pad, :] = jnp.zeros((pad, BRANCH), F32)
        h_ref[...] = jnp.zeros_like(h_ref)

    cbuf_ref[pad:pad + tt, :] = cc_ref[...].astype(F32) * cx_ref[...].astype(F32)
    conv = scw_ref[SCONV - 1:SCONV, :] * cbuf_ref[pad:pad + tt, :]
    for k in range(SCONV - 1):
        off = pad - (SCONV - 1) + k
        conv = conv + scw_ref[k:k + 1, :] * cbuf_ref[off:off + tt, :]
    cbuf_ref[0:pad, :] = cbuf_ref[tt:tt + pad, :]
    yc_ref[...] = (cb_ref[...].astype(F32) * conv * _silu(cg_ref[...].astype(F32))).astype(yc_ref.dtype)

    dbuf_ref[pad:pad + tt, :] = dx_ref[...].astype(F32)
    xs = lcb_ref[...] + lcw_ref[LRU_CONV - 1:LRU_CONV, :] * dbuf_ref[pad:pad + tt, :]
    for k in range(LRU_CONV - 1):
        off = pad - (LRU_CONV - 1) + k
        xs = xs + lcw_ref[k:k + 1, :] * dbuf_ref[off:off + tt, :]
    dbuf_ref[0:pad, :] = dbuf_ref[tt:tt + pad, :]
    xs_b = xs.astype(BF16)
    gates = [jnp.dot(xs_b[:, q * LRU_GROUP:(q + 1) * LRU_GROUP], wbd_ref[q], preferred_element_type=F32)
             for q in range(BRANCH // LRU_GROUP)]
    pre_a = jnp.concatenate([gq[:, 0:LRU_GROUP] for gq in gates], axis=1)
    pre_x = jnp.concatenate([gq[:, LRU_GROUP:2 * LRU_GROUP] for gq in gates], axis=1)
    rg = jax.nn.sigmoid(pre_a + ba_ref[...])
    ig = jax.nn.sigmoid(pre_x + bx_ref[...])
    log_a = -LRU_C * rg * _softplus(-lam_ref[...])
    a = jnp.exp(log_a)
    u = jnp.sqrt(jnp.tanh(-log_a) * (a * a + 1.0)) * (ig * xs)

    groups = tt // SUBLANES
    a3 = a.reshape(groups, SUBLANES, BRANCH)
    u3 = u.reshape(groups, SUBLANES, BRANCH)
    sub = lax.broadcasted_iota(jnp.int32, a3.shape, 1)
    k = 1
    while k < SUBLANES:
        keep = sub >= k
        a_s = jnp.where(keep, pltpu.roll(a3, k, 1), 1.0)
        u_s = jnp.where(keep, pltpu.roll(u3, k, 1), 0.0)
        u3 = a3 * u_s + u3
        a3 = a3 * a_s
        k *= 2
    carry = h_ref[0:1, :]
    hs = []
    for gi in range(groups):
        hg = u3[gi] + a3[gi] * carry
        carry = hg[SUBLANES - 1:SUBLANES, :]
        hs.append(hg)
    h_ref[...] = jnp.broadcast_to(carry, h_ref.shape)
    h = jnp.concatenate(hs, axis=0)
    yd_ref[...] = (h * _silu(dg_ref[...].astype(F32))).astype(yd_ref.dtype)


def _block_diag_pairs(w_a, w_x):
    per = LRU_GROUP // HEAD_DIM
    eye = jnp.eye(per, dtype=w_a.dtype)

    def bd(w):
        w4 = w.reshape(LRU_BLOCKS // per, per, HEAD_DIM, HEAD_DIM)
        return jnp.einsum('qiab,ij->qiajb', w4, eye).reshape(LRU_BLOCKS // per, LRU_GROUP, LRU_GROUP)

    return jnp.concatenate([bd(w_a), bd(w_x)], axis=2).astype(BF16)


def _cd(proj, p, bsz, seq, tt):
    m = proj.shape[0]
    per_b = seq // tt
    row_map = lambda b, t: b * per_b + t
    col = lambda cidx: pl.BlockSpec((tt, BRANCH), lambda b, t: (row_map(b, t), cidx))
    const = lambda shape: pl.BlockSpec(shape, lambda b, t: (0,) * len(shape))
    wbd = _block_diag_pairs(p["lru_w_a"], p["lru_w_x"])
    out_spec = pl.BlockSpec((tt, BRANCH), lambda b, t: (row_map(b, t), 0))
    vec = lambda a: a.reshape(1, -1)
    return pl.pallas_call(
        functools.partial(_cd_kernel, tt=tt),
        out_shape=(jax.ShapeDtypeStruct((m, BRANCH), BF16), jax.ShapeDtypeStruct((m, BRANCH), BF16)),
        grid=(bsz, per_b),
        in_specs=[col(COL_CB), col(COL_CC), col(COL_CX), col(COL_CG), col(COL_DX), col(COL_DG),
                  const((SCONV, BRANCH)), const((LRU_CONV, BRANCH)), const((1, BRANCH)),
                  const(wbd.shape), const((1, BRANCH)), const((1, BRANCH)), const((1, BRANCH))],
        out_specs=(out_spec, out_spec),
        scratch_shapes=[pltpu.VMEM((tt + SUBLANES, BRANCH), F32),
                        pltpu.VMEM((tt + SUBLANES, BRANCH), F32),
                        pltpu.VMEM((SUBLANES, BRANCH), F32)],
        compiler_params=_cparams(("parallel", "arbitrary")),
        name="sconv_rglru",
    )(proj, proj, proj, proj, proj, proj, p["sconv_w"], p["lru_conv_w"], vec(p["lru_conv_b"]), wbd,
      vec(p["lru_b_a"]), vec(p["lru_b_x"]), vec(p["lru_lambda"]))


def _merge_kernel(ya_ref, yb_ref, yc_ref, yd_ref, lg_ref, x_ref, gate_ref, bg_ref, wb_ref, wo_ref,
                  lnw_ref, lnb_ref, o_ref):
    m = None
    for k, y_ref in enumerate((ya_ref, yb_ref, yc_ref, yd_ref)):
        gk = jax.nn.sigmoid(lg_ref[:, k * D_MODEL:(k + 1) * D_MODEL].astype(F32) + bg_ref[k:k + 1, :])
        t = gk * jnp.dot(y_ref[...], wb_ref[k], preferred_element_type=F32)
        m = t if m is None else m + t
    out = jnp.dot(m.astype(BF16), wo_ref[...], preferred_element_type=F32)
    r = ALPHA * x_ref[...] + gate_ref[0] * out
    mu = jnp.mean(r, axis=-1, keepdims=True)
    rc = r - mu
    var = jnp.mean(jnp.square(rc), axis=-1, keepdims=True)
    o_ref[...] = rc * lax.rsqrt(var + LN_EPS) * lnw_ref[...] + lnb_ref[...]


def _merge(ya, yb, yc, yd, proj, x2, ada3, b_gate, w_branch, w_out, ln_w, ln_b, layer, seq, tm):
    m, d = x2.shape
    per_b = seq // tm
    ycol = pl.BlockSpec((tm, BRANCH), lambda i: (i, 0))
    once = pl.Buffered(1)
    return pl.pallas_call(
        _merge_kernel,
        out_shape=jax.ShapeDtypeStruct((m, d), F32),
        grid=(m // tm,),
        in_specs=[ycol, ycol, ycol, ycol,
                  pl.BlockSpec((tm, N_BRANCH * d), lambda i: (i, 0)),
                  pl.BlockSpec((tm, d), lambda i: (i, 0)),
                  pl.BlockSpec((1, 1, d), lambda i: (i // per_b, 0, 2)),
                  pl.BlockSpec((N_BRANCH, d), lambda i: (0, 0)),
                  pl.BlockSpec((None, N_BRANCH, BRANCH, d), lambda i: (layer, 0, 0, 0), pipeline_mode=once),
                  pl.BlockSpec((None, d, d), lambda i: (layer, 0, 0), pipeline_mode=once),
                  pl.BlockSpec((1, d), lambda i: (0, 0)),
                  pl.BlockSpec((1, d), lambda i: (0, 0))],
        out_specs=pl.BlockSpec((tm, d), lambda i: (i, 0)),
        compiler_params=_cparams(("parallel",)),
        name="merge",
    )(ya, yb, yc, yd, proj, x2, ada3, b_gate, w_branch, w_out, ln_w.reshape(1, -1), ln_b.reshape(1, -1))


def kernel(x, c, positions, w_ada, b_ada, w_in, b_gate, ssd_conv_w, ssd_conv_b, ssd_dt_bias, ssd_a_log,
           ssd_d, ssd_norm_w, attn_sinks, sconv_w, lru_conv_w, lru_conv_b, lru_w_a, lru_b_a, lru_w_x,
           lru_b_x, lru_lambda, w_branch, w_out, ln_w, ln_b):
    bsz, seq, d = x.shape
    small = dict(b_gate=b_gate, ssd_conv_w=ssd_conv_w, ssd_conv_b=ssd_conv_b, ssd_dt_bias=ssd_dt_bias,
                 ssd_a_log=ssd_a_log, ssd_d=ssd_d, ssd_norm_w=ssd_norm_w, attn_sinks=attn_sinks,
                 sconv_w=sconv_w, lru_conv_w=lru_conv_w, lru_conv_b=lru_conv_b, lru_w_a=lru_w_a,
                 lru_b_a=lru_b_a, lru_w_x=lru_w_x, lru_b_x=lru_b_x, lru_lambda=lru_lambda,
                 ln_w=ln_w, ln_b=ln_b)
    w_main, w_dt = _rearrange_w_in(w_in)
    w_branch_b = w_branch.astype(BF16)
    w_out_b = w_out.astype(BF16)
    cos_t, sin_t = _rope_tables(positions, tt=min(2048, bsz * seq))
    x2 = x.reshape(bsz * seq, d)
    for layer in range(DEPTH):
        p = {k: v[layer] for k, v in small.items()}
        ada3 = _ada(c, w_ada, b_ada, layer).reshape(bsz, 1, 3 * d)
        proj, dt_raw = _inproj(x2, ada3, w_main, w_dt, layer, seq, tm=min(1024, seq), tn=1536)
        ya = _ssd(proj, dt_raw, p, bsz, seq, tt=min(512, seq))
        yb = _swa(proj, cos_t, sin_t, p["attn_sinks"], bsz, seq, tt=min(512, seq))
        yc, yd = _cd(proj, p, bsz, seq, tt=min(256, seq))
        x2 = _merge(ya, yb, yc, yd, proj, x2, ada3, p["b_gate"], w_branch_b, w_out_b, p["ln_w"], p["ln_b"],
                    layer, seq, tm=min(256, seq))
    return x2.reshape(bsz, seq, d)
```

```python
import functools

import jax
import jax.numpy as jnp
from jax import lax
from jax.experimental import pallas as pl
from jax.experimental.pallas import tpu as pltpu

F32 = jnp.float32
BF16 = jnp.bfloat16

LANES = 128
SUBLANES = 8
VMEM_LIMIT = 56 * 1024 * 1024

D_MODEL = 2048
BRANCH = D_MODEL // 2
N_BRANCH = 4
HEAD_DIM = 64
SSD_HEADS = BRANCH // HEAD_DIM
SSD_GROUPS = 4
SSD_STATE = 128
SSD_CONV = 4
CHUNK = 128
ATTN_HEADS = BRANCH // HEAD_DIM
KV_HEADS = 4
KV_WIDTH = KV_HEADS * HEAD_DIM
WINDOW = 128
ROPE_THETA = 10000.0
SCONV = 3
LRU_BLOCKS = 16
LRU_CONV = 4
LRU_C = 8.0
LRU_GROUP = 256
LN_EPS = 1e-5
RMS_EPS = 1e-5
DEPTH = 2
ALPHA = (2.0 * DEPTH) ** 0.25
LOG2E = 1.4426950408889634

MERGE_COLS = N_BRANCH * D_MODEL
_C0 = MERGE_COLS // BRANCH
COL_Z, COL_XS, COL_BC, COL_Q, COL_GB = _C0, _C0 + 1, _C0 + 2, _C0 + 3, _C0 + 4
COL_CB, COL_CC, COL_CX, COL_CG, COL_DX, COL_DG = (_C0 + 5, _C0 + 6, _C0 + 7, _C0 + 8, _C0 + 9, _C0 + 10)
COL_KV = (MERGE_COLS + 11 * BRANCH) // (2 * KV_WIDTH)
ORIG_DT = 3 * BRANCH
ORIG_Q = ORIG_DT + SSD_HEADS
ORIG_K = ORIG_Q + BRANCH
ORIG_GB = ORIG_K + 2 * KV_WIDTH
ORIG_MERGE = ORIG_GB + 7 * BRANCH


IN_COLS = ORIG_MERGE + MERGE_COLS
MAIN_COLS = IN_COLS - SSD_HEADS
_W_PIECES = ((ORIG_MERGE, MERGE_COLS), (0, ORIG_DT), (ORIG_Q, BRANCH), (ORIG_GB, 7 * BRANCH),
             (ORIG_K, 2 * KV_WIDTH))


PREP_COLS = 512


def _rearrange_kernel(wt_ref, main_ref):
    main_ref[...] = wt_ref[...].T.astype(main_ref.dtype)


COL_UNIT = 16


def _source_column_units(j):
    dst = j * (PREP_COLS // COL_UNIT)
    src, off = None, 0
    for start, width in _W_PIECES:
        cand = dst - off // COL_UNIT + start // COL_UNIT
        src = cand if src is None else jnp.where(dst >= off // COL_UNIT, cand, src)
        off += width
    return src


def _rearrange_w_in(w_in):
    depth, d, n = w_in.shape
    w_main = pl.pallas_call(
        _rearrange_kernel,
        out_shape=jax.ShapeDtypeStruct((depth, d, MAIN_COLS), BF16),
        grid=(depth, MAIN_COLS // PREP_COLS),
        in_specs=[pl.BlockSpec((pl.Element(PREP_COLS), pl.Element(d)),
                               lambda l, j: ((l * (n // COL_UNIT) + _source_column_units(j)) * COL_UNIT, 0))],
        out_specs=pl.BlockSpec((None, d, PREP_COLS), lambda l, j: (l, 0, j)),
        compiler_params=_cparams(("parallel", "parallel")),
        name="w_in_layout",
    )(jnp.swapaxes(w_in, 1, 2).reshape(depth * n, d))
    w_dt = jnp.pad(w_in[..., ORIG_DT:ORIG_Q], ((0, 0), (0, 0), (0, LANES - SSD_HEADS))).astype(BF16)
    return w_main, w_dt


def _softplus(x):
    return jnp.maximum(x, 0.0) + jnp.log1p(jnp.exp(-jnp.abs(x)))


def _silu(x):
    return x * jax.nn.sigmoid(x)


def _cparams(sem):
    return pltpu.CompilerParams(dimension_semantics=sem, vmem_limit_bytes=VMEM_LIMIT)


def _ada_kernel(c_ref, w_ref, b_ref, o_ref):
    ca = _silu(c_ref[...])
    o_ref[...] = jnp.dot(ca, w_ref[...], preferred_element_type=F32,
                         precision=lax.Precision.HIGHEST) + b_ref[...]


def _ada(c, w_ada, b_ada, layer):
    bsz, d = c.shape
    n = w_ada.shape[-1]
    tn = 1536
    return pl.pallas_call(
        _ada_kernel,
        out_shape=jax.ShapeDtypeStruct((bsz, n), F32),
        grid=(n // tn,),
        in_specs=[pl.BlockSpec((bsz, d), lambda j: (0, 0)),
                  pl.BlockSpec((None, d, tn), lambda j: (layer, 0, j)),
                  pl.BlockSpec((None, 1, tn), lambda j: (layer, 0, j))],
        out_specs=pl.BlockSpec((bsz, tn), lambda j: (0, j)),
        compiler_params=_cparams(("arbitrary",)),
        name="ada",
    )(c, w_ada, b_ada.reshape(b_ada.shape[0], 1, n))


def _inproj_kernel(x_ref, shift_ref, scale_ref, w_ref, wdt_ref, o_ref, dt_ref, h_ref):
    @pl.when(pl.program_id(1) == 0)
    def _():
        h = x_ref[...] * (1.0 + scale_ref[0]) + shift_ref[0]
        hb = h.astype(BF16)
        h_ref[...] = hb
        dt_ref[...] = jnp.dot(hb, wdt_ref[...], preferred_element_type=F32)

    o_ref[...] = jnp.dot(h_ref[...], w_ref[...], preferred_element_type=F32).astype(o_ref.dtype)


def _inproj(x2, ada3, w_main, w_dt, layer, seq, tm, tn):
    m, d = x2.shape
    n = w_main.shape[-1]
    per_b = seq // tm
    return pl.pallas_call(
        _inproj_kernel,
        out_shape=(jax.ShapeDtypeStruct((m, n), BF16), jax.ShapeDtypeStruct((m, LANES), F32)),
        grid=(m // tm, n // tn),
        in_specs=[pl.BlockSpec((tm, d), lambda i, j: (i, 0)),
                  pl.BlockSpec((1, 1, d), lambda i, j: (i // per_b, 0, 0)),
                  pl.BlockSpec((1, 1, d), lambda i, j: (i // per_b, 0, 1)),
                  pl.BlockSpec((None, d, tn), lambda i, j: (layer, 0, j)),
                  pl.BlockSpec((None, d, LANES), lambda i, j: (layer, 0, 0))],
        out_specs=(pl.BlockSpec((tm, tn), lambda i, j: (i, j)),
                   pl.BlockSpec((tm, LANES), lambda i, j: (i, 0))),
        scratch_shapes=[pltpu.VMEM((tm, d), BF16)],
        compiler_params=_cparams(("parallel", "arbitrary")),
        name="inproj",
    )(x2, ada3, ada3, w_main, w_dt)


def _pair_expand(col_a, col_b, lo):
    shape = (col_a.shape[0], LANES)
    return jnp.where(lo, jnp.broadcast_to(col_a, shape), jnp.broadcast_to(col_b, shape))


def _ssd_kernel(z_ref, xs_ref, bc_ref, dt_ref, cw_ref, cb_ref, dtb_ref, aneg_ref, dsk_ref, nw_ref,
                o_ref, xbuf_ref, state_ref, *, n_chunks):
    pad = SUBLANES

    @pl.when(pl.program_id(1) == 0)
    def _():
        xbuf_ref[0:pad, :] = jnp.zeros((pad, 2 * BRANCH), F32)
        state_ref[...] = jnp.zeros_like(state_ref)

    row = lax.broadcasted_iota(jnp.int32, (CHUNK, CHUNK), 0)
    col = lax.broadcasted_iota(jnp.int32, (CHUNK, CHUNK), 1)
    causal = row >= col
    tril = causal.astype(F32)
    lo = col < HEAD_DIM

    def chunk_body(ci, carry):
        rows = pl.ds(pl.multiple_of(ci * CHUNK, CHUNK), CHUNK)
        xbuf_ref[pad:pad + CHUNK, 0:BRANCH] = xs_ref[rows, :].astype(F32)
        xbuf_ref[pad:pad + CHUNK, BRANCH:2 * BRANCH] = bc_ref[rows, :].astype(F32)
        conv = cb_ref[...] + cw_ref[SSD_CONV - 1:SSD_CONV, :] * xbuf_ref[pad:pad + CHUNK, :]
        for k in range(SSD_CONV - 1):
            off = pad - (SSD_CONV - 1) + k
            conv = conv + cw_ref[k:k + 1, :] * xbuf_ref[off:off + CHUNK, :]
        xbuf_ref[0:pad, :] = xbuf_ref[CHUNK:CHUNK + pad, :]
        xbc = _silu(conv)
        xc = xbc[:, 0:BRANCH]

        dt = _softplus(dt_ref[rows, :] + dtb_ref[...])
        da = dt * aneg_ref[...]
        cum = LOG2E * jnp.dot(tril, da, preferred_element_type=F32, precision=lax.Precision.HIGHEST)
        cum_t = cum.T

        ys = []
        for g in range(SSD_GROUPS):
            b_g = xbc[:, BRANCH + g * SSD_STATE:BRANCH + (g + 1) * SSD_STATE]
            c_g = xbc[:, BRANCH + (SSD_GROUPS + g) * SSD_STATE:BRANCH + (SSD_GROUPS + g + 1) * SSD_STATE]
            b_gb = b_g.astype(BF16)
            c_gb = c_g.astype(BF16)
            cb = lax.dot_general(c_gb, b_gb, (((1,), (1,)), ((), ())), preferred_element_type=F32)
            b_t = b_g.T.astype(BF16)
            for jp in range(2):
                j = 2 * g + jp
                h0, h1 = 2 * j, 2 * j + 1
                x_p = xc[:, j * LANES:(j + 1) * LANES]
                cum0 = jnp.broadcast_to(cum[:, h0:h0 + 1], (CHUNK, CHUNK))
                cum1 = jnp.broadcast_to(cum[:, h1:h1 + 1], (CHUNK, CHUNK))
                cum_p = jnp.where(lo, cum0, cum1)
                dt_p = _pair_expand(dt[:, h0:h0 + 1], dt[:, h1:h1 + 1], lo)
                xdt = x_p * dt_p
                xdt_b = xdt.astype(BF16)
                l0 = jnp.exp2(jnp.where(causal, cum0 - cum_t[h0:h0 + 1, :], -jnp.inf))
                l1 = jnp.exp2(jnp.where(causal, cum1 - cum_t[h1:h1 + 1, :], -jnp.inf))
                y = jnp.where(lo, jnp.dot((cb * l0).astype(BF16), xdt_b, preferred_element_type=F32),
                              jnp.dot((cb * l1).astype(BF16), xdt_b, preferred_element_type=F32))
                st = state_ref[:, j * LANES:(j + 1) * LANES]
                y_off = jnp.dot(c_gb, st.astype(BF16), preferred_element_type=F32)
                y = y + y_off * jnp.exp2(cum_p)
                y = y + x_p * dsk_ref[:, j * LANES:(j + 1) * LANES]
                ys.append(y)
                cum_last = cum_p[CHUNK - 1:CHUNK, :]
                xw = (xdt * jnp.exp2(cum_last - cum_p)).astype(BF16)
                s_new = jnp.dot(b_t, xw, preferred_element_type=F32)
                state_ref[:, j * LANES:(j + 1) * LANES] = st * jnp.exp2(cum_last) + s_new
        y = jnp.concatenate(ys, axis=1)
        y = y * _silu(z_ref[rows, :].astype(F32))
        y = y * lax.rsqrt(jnp.mean(jnp.square(y), axis=-1, keepdims=True) + RMS_EPS)
        o_ref[rows, :] = (y * nw_ref[...]).astype(o_ref.dtype)
        return carry

    lax.fori_loop(0, n_chunks, chunk_body, 0)


def _ssd(proj, dt_raw, p, bsz, seq, tt):
    m = proj.shape[0]
    per_b = seq // tt
    row_map = lambda b, t: b * per_b + t
    pad_heads = LANES - SSD_HEADS
    dtb = jnp.pad(p["ssd_dt_bias"], (0, pad_heads)).reshape(1, LANES)
    aneg = jnp.pad(-jnp.exp(p["ssd_a_log"]), (0, pad_heads)).reshape(1, LANES)
    dsk = jnp.repeat(p["ssd_d"], HEAD_DIM).reshape(1, BRANCH)
    const = lambda shape: pl.BlockSpec(shape, lambda b, t: (0, 0))
    col = lambda cidx: pl.BlockSpec((tt, BRANCH), lambda b, t: (row_map(b, t), cidx))
    return pl.pallas_call(
        functools.partial(_ssd_kernel, n_chunks=tt // CHUNK),
        out_shape=jax.ShapeDtypeStruct((m, BRANCH), BF16),
        grid=(bsz, per_b),
        in_specs=[col(COL_Z), col(COL_XS), col(COL_BC),
                  pl.BlockSpec((tt, LANES), lambda b, t: (row_map(b, t), 0)),
                  const((SSD_CONV, 2 * BRANCH)), const((1, 2 * BRANCH)),
                  const((1, LANES)), const((1, LANES)), const((1, BRANCH)), const((1, BRANCH))],
        out_specs=pl.BlockSpec((tt, BRANCH), lambda b, t: (row_map(b, t), 0)),
        scratch_shapes=[pltpu.VMEM((CHUNK + SUBLANES, 2 * BRANCH), F32),
                        pltpu.VMEM((SSD_STATE, BRANCH), F32)],
        compiler_params=_cparams(("parallel", "arbitrary")),
        name="ssd",
    )(proj, proj, proj, dt_raw, p["ssd_conv_w"], p["ssd_conv_b"].reshape(1, -1), dtb, aneg, dsk,
      p["ssd_norm_w"].reshape(1, -1))


def _rope_table_kernel(pos_ref, invf_ref, cos_ref, sin_ref):
    ang = pos_ref[...].astype(F32) * invf_ref[...]
    lane = lax.broadcasted_iota(jnp.int32, ang.shape, 1)
    first_half = (lane % HEAD_DIM) < (HEAD_DIM // 2)
    cos_ref[...] = jnp.cos(ang)
    sin = jnp.sin(ang)
    sin_ref[...] = jnp.where(first_half, -sin, sin)


def _rope_tables(positions, tt):
    half = HEAD_DIM // 2
    inv_half = ROPE_THETA ** (-jnp.arange(half, dtype=F32) / half)
    inv_freq = jnp.tile(inv_half, LANES // half).reshape(1, LANES)
    m = positions.size
    shape = jax.ShapeDtypeStruct((m, LANES), F32)
    return pl.pallas_call(
        _rope_table_kernel,
        out_shape=(shape, shape),
        grid=(m // tt,),
        in_specs=[pl.BlockSpec((tt, 1), lambda i: (i, 0)), pl.BlockSpec((1, LANES), lambda i: (0, 0))],
        out_specs=(pl.BlockSpec((tt, LANES), lambda i: (i, 0)), pl.BlockSpec((tt, LANES), lambda i: (i, 0))),
        compiler_params=_cparams(("parallel",)),
        name="rope_tables",
    )(positions.reshape(m, 1), inv_freq)


def _rope_apply(x, cos, sin_signed, first_half):
    half = HEAD_DIM // 2
    outs = []
    for cblk in range(x.shape[1] // LANES):
        xb = x[:, cblk * LANES:(cblk + 1) * LANES]
        rot = jnp.where(first_half, pltpu.roll(xb, LANES - half, 1), pltpu.roll(xb, half, 1))
        outs.append(xb * cos + rot * sin_signed)
    return outs


def _swa_kernel(q_ref, kv_ref, g_ref, cos_ref, sin_ref, sink_ref, o_ref,
                kcat_ref, vtcat_ref, qb_ref, s_ref, e_ref, inv_ref, *, n_blocks):
    tile = pl.program_id(1)

    @pl.when(tile == 0)
    def _():
        kcat_ref[...] = jnp.zeros_like(kcat_ref)
        vtcat_ref[...] = jnp.zeros_like(vtcat_ref)

    row = lax.broadcasted_iota(jnp.int32, (WINDOW, WINDOW), 0)
    lane = lax.broadcasted_iota(jnp.int32, (WINDOW, WINDOW), 1)
    lo = lane < HEAD_DIM
    top = row < HEAD_DIM
    first_half = (lane % HEAD_DIM) < (HEAD_DIM // 2)
    key = lax.broadcasted_iota(jnp.int32, (2 * WINDOW, WINDOW), 0)
    qry = lax.broadcasted_iota(jnp.int32, (2 * WINDOW, WINDOW), 1)
    band = jnp.logical_and(key > qry, key - WINDOW <= qry)
    nt = (((1,), (1,)), ((), ()))
    n_pairs = ATTN_HEADS // 2

    def block_body(bi, carry):
        rows = pl.ds(pl.multiple_of(bi * WINDOW, WINDOW), WINDOW)
        valid = jnp.logical_and(band, jnp.logical_or(key >= WINDOW, tile * n_blocks + bi > 0))
        cos = cos_ref[rows, :]
        sin_s = sin_ref[rows, :]
        q_blocks = _rope_apply(q_ref[rows, :].astype(F32) * (LOG2E * HEAD_DIM ** -0.5), cos, sin_s,
                               first_half)
        for j in range(n_pairs):
            qb_ref[j] = q_blocks[j].astype(BF16)
        kv = kv_ref[rows, :].astype(F32)
        k_blocks = _rope_apply(kv[:, 0:KV_WIDTH], cos, sin_s, first_half)

        for g in range(KV_HEADS):
            kp = k_blocks[g // 2]
            vp = kv[:, KV_WIDTH + (g // 2) * LANES:KV_WIDTH + (g // 2 + 1) * LANES]
            kr = pltpu.roll(kp, HEAD_DIM, 1)
            vr = pltpu.roll(vp, HEAD_DIM, 1)
            own_lo = (g % 2) == 0
            kcat_ref[2 * g, WINDOW:, :] = jnp.where(lo, kp if own_lo else kr, 0.0).astype(BF16)
            kcat_ref[2 * g + 1, WINDOW:, :] = jnp.where(lo, 0.0, kr if own_lo else kp).astype(BF16)
            v_t = jnp.where(lo, vp if own_lo else vr, vr if own_lo else vp).T
            vtcat_ref[2 * g, :, WINDOW:] = jnp.where(top, v_t, 0.0).astype(BF16)
            vtcat_ref[2 * g + 1, :, WINDOW:] = jnp.where(top, 0.0, v_t).astype(BF16)

        for j in range(n_pairs):
            for hh in range(2):
                s_ref[2 * j + hh] = lax.dot_general(kcat_ref[2 * (j // 2) + hh], qb_ref[j], nt,
                                                    preferred_element_type=F32)
        for h in range(ATTN_HEADS):
            s = jnp.where(valid, s_ref[h], -jnp.inf)
            sink = LOG2E * sink_ref[h:h + 1, :]
            mx = jnp.maximum(s.max(0, keepdims=True), sink)
            e = jnp.exp2(s - mx)
            inv_ref[h:h + 1, :] = 1.0 / (e.sum(0, keepdims=True) + jnp.exp2(sink - mx))
            e_ref[h] = e.astype(BF16)
        outs = []
        for j in range(n_pairs):
            g = j // 2
            acc = jnp.dot(vtcat_ref[2 * g], e_ref[2 * j], preferred_element_type=F32)
            acc = acc + jnp.dot(vtcat_ref[2 * g + 1], e_ref[2 * j + 1], preferred_element_type=F32)
            inv = jnp.where(top, inv_ref[2 * j:2 * j + 1, :], inv_ref[2 * j + 1:2 * j + 2, :])
            outs.append((acc * inv).T)
        o = jnp.concatenate(outs, axis=1)
        o_ref[rows, :] = (o * _silu(g_ref[rows, :].astype(F32))).astype(o_ref.dtype)

        for i in range(2 * KV_HEADS):
            kcat_ref[i, :WINDOW, :] = kcat_ref[i, WINDOW:, :]
            vtcat_ref[i, :, :WINDOW] = vtcat_ref[i, :, WINDOW:]
        return carry

    lax.fori_loop(0, n_blocks, block_body, 0, unroll=2)


def _swa(proj, cos_t, sin_t, sinks, bsz, seq, tt):
    m = proj.shape[0]
    per_b = seq // tt
    row_map = lambda b, t: b * per_b + t
    sinks_b = jnp.broadcast_to(sinks.reshape(ATTN_HEADS, 1), (ATTN_HEADS, LANES))
    return pl.pallas_call(
        functools.partial(_swa_kernel, n_blocks=tt // WINDOW),
        out_shape=jax.ShapeDtypeStruct((m, BRANCH), BF16),
        grid=(bsz, per_b),
        in_specs=[pl.BlockSpec((tt, BRANCH), lambda b, t: (row_map(b, t), COL_Q)),
                  pl.BlockSpec((tt, 2 * KV_WIDTH), lambda b, t: (row_map(b, t), COL_KV)),
                  pl.BlockSpec((tt, BRANCH), lambda b, t: (row_map(b, t), COL_GB)),
                  pl.BlockSpec((tt, LANES), lambda b, t: (row_map(b, t), 0)),
                  pl.BlockSpec((tt, LANES), lambda b, t: (row_map(b, t), 0)),
                  pl.BlockSpec((ATTN_HEADS, LANES), lambda b, t: (0, 0))],
        out_specs=pl.BlockSpec((tt, BRANCH), lambda b, t: (row_map(b, t), 0)),
        scratch_shapes=[pltpu.VMEM((2 * KV_HEADS, 2 * WINDOW, LANES), BF16),
                        pltpu.VMEM((2 * KV_HEADS, LANES, 2 * WINDOW), BF16),
                        pltpu.VMEM((ATTN_HEADS // 2, WINDOW, LANES), BF16),
                        pltpu.VMEM((ATTN_HEADS, 2 * WINDOW, WINDOW), F32),
                        pltpu.VMEM((ATTN_HEADS, 2 * WINDOW, WINDOW), BF16),
                        pltpu.VMEM((ATTN_HEADS, LANES), F32)],
        compiler_params=_cparams(("parallel", "arbitrary")),
        name="swa",
    )(proj, proj, proj, cos_t, sin_t, sinks_b)


def _cd_kernel(cb_ref, cc_ref, cx_ref, cg_ref, dx_ref, dg_ref, scw_ref, lcw_ref, lcb_ref, wbd_ref,
               ba_ref, bx_ref, lam_ref, yc_ref, yd_ref, cbuf_ref, dbuf_ref, h_ref, *, tt):
    pad = SUBLANES

    @pl.when(pl.program_id(1) == 0)
    def _():
        cbuf_ref[0:pad, :] = jnp.zeros((pad, BRANCH), F32)
        dbuf_ref[0:pad, :] = jnp.zeros((pad, BRANCH), F32)
        h_ref[...] = jnp.zeros_like(h_ref)

    cbuf_ref[pad:pad + tt, :] = cc_ref[...].astype(F32) * cx_ref[...].astype(F32)
    conv = scw_ref[SCONV - 1:SCONV, :] * cbuf_ref[pad:pad + tt, :]
    for k in range(SCONV - 1):
        off = pad - (SCONV - 1) + k
        conv = conv + scw_ref[k:k + 1, :] * cbuf_ref[off:off + tt, :]
    cbuf_ref[0:pad, :] = cbuf_ref[tt:tt + pad, :]
    yc_ref[...] = (cb_ref[...].astype(F32) * conv * _silu(cg_ref[...].astype(F32))).astype(yc_ref.dtype)

    dbuf_ref[pad:pad + tt, :] = dx_ref[...].astype(F32)
    xs = lcb_ref[...] + lcw_ref[LRU_CONV - 1:LRU_CONV, :] * dbuf_ref[pad:pad + tt, :]
    for k in range(LRU_CONV - 1):
        off = pad - (LRU_CONV - 1) + k
        xs = xs + lcw_ref[k:k + 1, :] * dbuf_ref[off:off + tt, :]
    dbuf_ref[0:pad, :] = dbuf_ref[tt:tt + pad, :]
    xs_b = xs.astype(BF16)
    gates = [jnp.dot(xs_b[:, q * LRU_GROUP:(q + 1) * LRU_GROUP], wbd_ref[q], preferred_element_type=F32)
             for q in range(BRANCH // LRU_GROUP)]
    pre_a = jnp.concatenate([gq[:, 0:LRU_GROUP] for gq in gates], axis=1)
    pre_x = jnp.concatenate([gq[:, LRU_GROUP:2 * LRU_GROUP] for gq in gates], axis=1)
    rg = jax.nn.sigmoid(pre_a + ba_ref[...])
    ig = jax.nn.sigmoid(pre_x + bx_ref[...])
    log_a = -LRU_C * rg * _softplus(-lam_ref[...])
    a = jnp.exp(log_a)
    u = jnp.sqrt(jnp.tanh(-log_a) * (a * a + 1.0)) * (ig * xs)

    groups = tt // SUBLANES
    a3 = a.reshape(groups, SUBLANES, BRANCH)
    u3 = u.reshape(groups, SUBLANES, BRANCH)
    sub = lax.broadcasted_iota(jnp.int32, a3.shape, 1)
    k = 1
    while k < SUBLANES:
        keep = sub >= k
        a_s = jnp.where(keep, pltpu.roll(a3, k, 1), 1.0)
        u_s = jnp.where(keep, pltpu.roll(u3, k, 1), 0.0)
        u3 = a3 * u_s + u3
        a3 = a3 * a_s
        k *= 2
    carry = h_ref[0:1, :]
    hs = []
    for gi in range(groups):
        hg = u3[gi] + a3[gi] * carry
        carry = hg[SUBLANES - 1:SUBLANES, :]
        hs.append(hg)
    h_ref[...] = jnp.broadcast_to(carry, h_ref.shape)
    h = jnp.concatenate(hs, axis=0)
    yd_ref[...] = (h * _silu(dg_ref[...].astype(F32))).astype(yd_ref.dtype)


def _block_diag_pairs(w_a, w_x):
    per = LRU_GROUP // HEAD_DIM
    eye = jnp.eye(per, dtype=w_a.dtype)

    def bd(w):
        w4 = w.reshape(LRU_BLOCKS // per, per, HEAD_DIM, HEAD_DIM)
        return jnp.einsum('qiab,ij->qiajb', w4, eye).reshape(LRU_BLOCKS // per, LRU_GROUP, LRU_GROUP)

    return jnp.concatenate([bd(w_a), bd(w_x)], axis=2).astype(BF16)


def _cd(proj, p, bsz, seq, tt):
    m = proj.shape[0]
    per_b = seq // tt
    row_map = lambda b, t: b * per_b + t
    col = lambda cidx: pl.BlockSpec((tt, BRANCH), lambda b, t: (row_map(b, t), cidx))
    const = lambda shape: pl.BlockSpec(shape, lambda b, t: (0,) * len(shape))
    wbd = _block_diag_pairs(p["lru_w_a"], p["lru_w_x"])
    out_spec = pl.BlockSpec((tt, BRANCH), lambda b, t: (row_map(b, t), 0))
    vec = lambda a: a.reshape(1, -1)
    return pl.pallas_call(
        functools.partial(_cd_kernel, tt=tt),
        out_shape=(jax.ShapeDtypeStruct((m, BRANCH), BF16), jax.ShapeDtypeStruct((m, BRANCH), BF16)),
        grid=(bsz, per_b),
        in_specs=[col(COL_CB), col(COL_CC), col(COL_CX), col(COL_CG), col(COL_DX), col(COL_DG),
                  const((SCONV, BRANCH)), const((LRU_CONV, BRANCH)), const((1, BRANCH)),
                  const(wbd.shape), const((1, BRANCH)), const((1, BRANCH)), const((1, BRANCH))],
        out_specs=(out_spec, out_spec),
        scratch_shapes=[pltpu.VMEM((tt + SUBLANES, BRANCH), F32),
                        pltpu.VMEM((tt + SUBLANES, BRANCH), F32),
                        pltpu.VMEM((SUBLANES, BRANCH), F32)],
        compiler_params=_cparams(("parallel", "arbitrary")),
        name="sconv_rglru",
    )(proj, proj, proj, proj, proj, proj, p["sconv_w"], p["lru_conv_w"], vec(p["lru_conv_b"]), wbd,
      vec(p["lru_b_a"]), vec(p["lru_b_x"]), vec(p["lru_lambda"]))


def _merge_kernel(ya_ref, yb_ref, yc_ref, yd_ref, lg_ref, x_ref, gate_ref, bg_ref, wb_ref, wo_ref,
                  lnw_ref, lnb_ref, o_ref):
    m = None
    for k, y_ref in enumerate((ya_ref, yb_ref, yc_ref, yd_ref)):
        gk = jax.nn.sigmoid(lg_ref[:, k * D_MODEL:(k + 1) * D_MODEL].astype(F32) + bg_ref[k:k + 1, :])
        t = gk * jnp.dot(y_ref[...], wb_ref[k], preferred_element_type=F32)
        m = t if m is None else m + t
    out = jnp.dot(m.astype(BF16), wo_ref[...], preferred_element_type=F32)
    r = ALPHA * x_ref[...] + gate_ref[0] * out
    mu = jnp.mean(r, axis=-1, keepdims=True)
    rc = r - mu
    var = jnp.mean(jnp.square(rc), axis=-1, keepdims=True)
    o_ref[...] = rc * lax.rsqrt(var + LN_EPS) * lnw_ref[...] + lnb_ref[...]


def _merge(ya, yb, yc, yd, proj, x2, ada3, b_gate, w_branch, w_out, ln_w, ln_b, layer, seq, tm):
    m, d = x2.shape
    per_b = seq // tm
    ycol = pl.BlockSpec((tm, BRANCH), lambda i: (i, 0))
    once = pl.Buffered(1)
    return pl.pallas_call(
        _merge_kernel,
        out_shape=jax.ShapeDtypeStruct((m, d), F32),
        grid=(m // tm,),
        in_specs=[ycol, ycol, ycol, ycol,
                  pl.BlockSpec((tm, N_BRANCH * d), lambda i: (i, 0)),
                  pl.BlockSpec((tm, d), lambda i: (i, 0)),
                  pl.BlockSpec((1, 1, d), lambda i: (i // per_b, 0, 2)),
                  pl.BlockSpec((N_BRANCH, d), lambda i: (0, 0)),
                  pl.BlockSpec((None, N_BRANCH, BRANCH, d), lambda i: (layer, 0, 0, 0), pipeline_mode=once),
                  pl.BlockSpec((None, d, d), lambda i: (layer, 0, 0), pipeline_mode=once),
                  pl.BlockSpec((1, d), lambda i: (0, 0)),
                  pl.BlockSpec((1, d), lambda i: (0, 0))],
        out_specs=pl.BlockSpec((tm, d), lambda i: (i, 0)),
        compiler_params=_cparams(("parallel",)),
        name="merge",
    )(ya, yb, yc, yd, proj, x2, ada3, b_gate, w_branch, w_out, ln_w.reshape(1, -1), ln_b.reshape(1, -1))


def kernel(x, c, positions, w_ada, b_ada, w_in, b_gate, ssd_conv_w, ssd_conv_b, ssd_dt_bias, ssd_a_log,
           ssd_d, ssd_norm_w, attn_sinks, sconv_w, lru_conv_w, lru_conv_b, lru_w_a, lru_b_a, lru_w_x,
           lru_b_x, lru_lambda, w_branch, w_out, ln_w, ln_b):
    bsz, seq, d = x.shape
    small = dict(b_gate=b_gate, ssd_conv_w=ssd_conv_w, ssd_conv_b=ssd_conv_b, ssd_dt_bias=ssd_dt_bias,
                 ssd_a_log=ssd_a_log, ssd_d=ssd_d, ssd_norm_w=ssd_norm_w, attn_sinks=attn_sinks,
                 sconv_w=sconv_w, lru_conv_w=lru_conv_w, lru_conv_b=lru_conv_b, lru_w_a=lru_w_a,
                 lru_b_a=lru_b_a, lru_w_x=lru_w_x, lru_b_x=lru_b_x, lru_lambda=lru_lambda,
                 ln_w=ln_w, ln_b=ln_b)
    w_main, w_dt = _rearrange_w_in(w_in)
    w_branch_b = w_branch.astype(BF16)
    w_out_b = w_out.astype(BF16)
    cos_t, sin_t = _rope_tables(positions, tt=min(2048, bsz * seq))
    x2 = x.reshape(bsz * seq, d)
    for layer in range(DEPTH):
        p = {k: v[layer] for k, v in small.items()}
        ada3 = _ada(c, w_ada, b_ada, layer).reshape(bsz, 1, 3 * d)
        proj, dt_raw = _inproj(x2, ada3, w_main, w_dt, layer, seq, tm=min(1024, seq), tn=1536)
        ya = _ssd(proj, dt_raw, p, bsz, seq, tt=min(1024, seq))
        yb = _swa(proj, cos_t, sin_t, p["attn_sinks"], bsz, seq, tt=min(1024, seq))
        yc, yd = _cd(proj, p, bsz, seq, tt=min(512, seq))
        x2 = _merge(ya, yb, yc, yd, proj, x2, ada3, p["b_gate"], w_branch_b, w_out_b, p["ln_w"], p["ln_b"],
                    layer, seq, tm=min(256, seq))
    return x2.reshape(bsz, seq, d)
```

```python
import functools

import jax
import jax.numpy as jnp
from jax import lax
from jax.experimental import pallas as pl
from jax.experimental.pallas import tpu as pltpu

F32 = jnp.float32
BF16 = jnp.bfloat16

LANES = 128
SUBLANES = 8
VMEM_LIMIT = 56 * 1024 * 1024

D_MODEL = 2048
BRANCH = D_MODEL // 2
N_BRANCH = 4
HEAD_DIM = 64
SSD_HEADS = BRANCH // HEAD_DIM
SSD_GROUPS = 4
SSD_STATE = 128
SSD_CONV = 4
CHUNK = 128
ATTN_HEADS = BRANCH // HEAD_DIM
KV_HEADS = 4
KV_WIDTH = KV_HEADS * HEAD_DIM
WINDOW = 128
ROPE_THETA = 10000.0
SCONV = 3
LRU_BLOCKS = 16
LRU_CONV = 4
LRU_C = 8.0
LRU_GROUP = 256
LN_EPS = 1e-5
RMS_EPS = 1e-5
DEPTH = 2
ALPHA = (2.0 * DEPTH) ** 0.25
LOG2E = 1.4426950408889634

MERGE_COLS = N_BRANCH * D_MODEL
_C0 = MERGE_COLS // BRANCH
COL_Z, COL_XS, COL_BC, COL_Q, COL_GB = _C0, _C0 + 1, _C0 + 2, _C0 + 3, _C0 + 4
COL_CB, COL_CC, COL_CX, COL_CG, COL_DX, COL_DG = (_C0 + 5, _C0 + 6, _C0 + 7, _C0 + 8, _C0 + 9, _C0 + 10)
COL_KV = (MERGE_COLS + 11 * BRANCH) // (2 * KV_WIDTH)
ORIG_DT = 3 * BRANCH
ORIG_Q = ORIG_DT + SSD_HEADS
ORIG_K = ORIG_Q + BRANCH
ORIG_GB = ORIG_K + 2 * KV_WIDTH
ORIG_MERGE = ORIG_GB + 7 * BRANCH


IN_COLS = ORIG_MERGE + MERGE_COLS
MAIN_COLS = IN_COLS - SSD_HEADS
_W_PIECES = ((ORIG_MERGE, MERGE_COLS), (0, ORIG_DT), (ORIG_Q, BRANCH), (ORIG_GB, 7 * BRANCH),
             (ORIG_K, 2 * KV_WIDTH))


PREP_COLS = 512


def _rearrange_kernel(wt_ref, main_ref):
    main_ref[...] = wt_ref[...].T.astype(main_ref.dtype)


COL_UNIT = 16


def _source_column_units(j):
    dst = j * (PREP_COLS // COL_UNIT)
    src, off = None, 0
    for start, width in _W_PIECES:
        cand = dst - off // COL_UNIT + start // COL_UNIT
        src = cand if src is None else jnp.where(dst >= off // COL_UNIT, cand, src)
        off += width
    return src


def _rearrange_w_in(w_in):
    depth, d, n = w_in.shape
    w_main = pl.pallas_call(
        _rearrange_kernel,
        out_shape=jax.ShapeDtypeStruct((depth, d, MAIN_COLS), BF16),
        grid=(depth, MAIN_COLS // PREP_COLS),
        in_specs=[pl.BlockSpec((pl.Element(PREP_COLS), pl.Element(d)),
                               lambda l, j: ((l * (n // COL_UNIT) + _source_column_units(j)) * COL_UNIT, 0))],
        out_specs=pl.BlockSpec((None, d, PREP_COLS), lambda l, j: (l, 0, j)),
        compiler_params=_cparams(("parallel", "parallel")),
        name="w_in_layout",
    )(jnp.swapaxes(w_in, 1, 2).reshape(depth * n, d))
    w_dt = jnp.pad(w_in[..., ORIG_DT:ORIG_Q], ((0, 0), (0, 0), (0, LANES - SSD_HEADS))).astype(BF16)
    return w_main, w_dt


def _softplus(x):
    return jnp.maximum(x, 0.0) + jnp.log1p(jnp.exp(-jnp.abs(x)))


def _silu(x):
    return x * jax.nn.sigmoid(x)


def _cparams(sem):
    return pltpu.CompilerParams(dimension_semantics=sem, vmem_limit_bytes=VMEM_LIMIT)


def _ada_kernel(c_ref, w_ref, b_ref, o_ref):
    ca = _silu(c_ref[...])
    o_ref[...] = jnp.dot(ca, w_ref[...], preferred_element_type=F32,
                         precision=lax.Precision.HIGHEST) + b_ref[...]


def _ada(c, w_ada, b_ada, layer):
    bsz, d = c.shape
    n = w_ada.shape[-1]
    tn = 768
    return pl.pallas_call(
        _ada_kernel,
        out_shape=jax.ShapeDtypeStruct((bsz, n), F32),
        grid=(n // tn,),
        in_specs=[pl.BlockSpec((bsz, d), lambda j: (0, 0)),
                  pl.BlockSpec((None, d, tn), lambda j: (layer, 0, j)),
                  pl.BlockSpec((None, 1, tn), lambda j: (layer, 0, j))],
        out_specs=pl.BlockSpec((bsz, tn), lambda j: (0, j)),
        compiler_params=_cparams(("arbitrary",)),
        name="ada",
    )(c, w_ada, b_ada.reshape(b_ada.shape[0], 1, n))


def _inproj_kernel(x_ref, shift_ref, scale_ref, w_ref, wdt_ref, o_ref, dt_ref, h_ref):
    @pl.when(pl.program_id(1) == 0)
    def _():
        h = x_ref[...] * (1.0 + scale_ref[0]) + shift_ref[0]
        hb = h.astype(BF16)
        h_ref[...] = hb
        dt_ref[...] = jnp.dot(hb, wdt_ref[...], preferred_element_type=F32)

    o_ref[...] = jnp.dot(h_ref[...], w_ref[...], preferred_element_type=F32).astype(o_ref.dtype)


def _inproj(x2, ada3, w_main, w_dt, layer, seq, tm, tn):
    m, d = x2.shape
    n = w_main.shape[-1]
    per_b = seq // tm
    return pl.pallas_call(
        _inproj_kernel,
        out_shape=(jax.ShapeDtypeStruct((m, n), BF16), jax.ShapeDtypeStruct((m, LANES), F32)),
        grid=(m // tm, n // tn),
        in_specs=[pl.BlockSpec((tm, d), lambda i, j: (i, 0)),
                  pl.BlockSpec((1, 1, d), lambda i, j: (i // per_b, 0, 0)),
                  pl.BlockSpec((1, 1, d), lambda i, j: (i // per_b, 0, 1)),
                  pl.BlockSpec((None, d, tn), lambda i, j: (layer, 0, j)),
                  pl.BlockSpec((None, d, LANES), lambda i, j: (layer, 0, 0))],
        out_specs=(pl.BlockSpec((tm, tn), lambda i, j: (i, j)),
                   pl.BlockSpec((tm, LANES), lambda i, j: (i, 0))),
        scratch_shapes=[pltpu.VMEM((tm, d), BF16)],
        compiler_params=_cparams(("parallel", "arbitrary")),
        name="inproj",
    )(x2, ada3, ada3, w_main, w_dt)


def _pair_expand(col_a, col_b, lo):
    shape = (col_a.shape[0], LANES)
    return jnp.where(lo, jnp.broadcast_to(col_a, shape), jnp.broadcast_to(col_b, shape))


def _ssd_kernel(z_ref, xs_ref, bc_ref, dt_ref, cw_ref, cb_ref, dtb_ref, aneg_ref, dsk_ref, nw_ref,
                o_ref, xbuf_ref, state_ref, *, n_chunks):
    pad = SUBLANES

    @pl.when(pl.program_id(1) == 0)
    def _():
        xbuf_ref[0:pad, :] = jnp.zeros((pad, 2 * BRANCH), F32)
        state_ref[...] = jnp.zeros_like(state_ref)

    row = lax.broadcasted_iota(jnp.int32, (CHUNK, CHUNK), 0)
    col = lax.broadcasted_iota(jnp.int32, (CHUNK, CHUNK), 1)
    causal = row >= col
    tril = causal.astype(F32)
    lo = col < HEAD_DIM

    def chunk_body(ci, carry):
        rows = pl.ds(pl.multiple_of(ci * CHUNK, CHUNK), CHUNK)
        xbuf_ref[pad:pad + CHUNK, 0:BRANCH] = xs_ref[rows, :].astype(F32)
        xbuf_ref[pad:pad + CHUNK, BRANCH:2 * BRANCH] = bc_ref[rows, :].astype(F32)
        conv = cb_ref[...] + cw_ref[SSD_CONV - 1:SSD_CONV, :] * xbuf_ref[pad:pad + CHUNK, :]
        for k in range(SSD_CONV - 1):
            off = pad - (SSD_CONV - 1) + k
            conv = conv + cw_ref[k:k + 1, :] * xbuf_ref[off:off + CHUNK, :]
        xbuf_ref[0:pad, :] = xbuf_ref[CHUNK:CHUNK + pad, :]
        xbc = _silu(conv)
        xc = xbc[:, 0:BRANCH]

        dt = _softplus(dt_ref[rows, :] + dtb_ref[...])
        da = dt * aneg_ref[...]
        cum = LOG2E * jnp.dot(tril, da, preferred_element_type=F32, precision=lax.Precision.HIGHEST)
        cum_t = cum.T

        ys = []
        for g in range(SSD_GROUPS):
            b_g = xbc[:, BRANCH + g * SSD_STATE:BRANCH + (g + 1) * SSD_STATE]
            c_g = xbc[:, BRANCH + (SSD_GROUPS + g) * SSD_STATE:BRANCH + (SSD_GROUPS + g + 1) * SSD_STATE]
            b_gb = b_g.astype(BF16)
            c_gb = c_g.astype(BF16)
            cb = lax.dot_general(c_gb, b_gb, (((1,), (1,)), ((), ())), preferred_element_type=F32)
            b_t = b_g.T.astype(BF16)
            for jp in range(2):
                j = 2 * g + jp
                h0, h1 = 2 * j, 2 * j + 1
                x_p = xc[:, j * LANES:(j + 1) * LANES]
                cum0 = jnp.broadcast_to(cum[:, h0:h0 + 1], (CHUNK, CHUNK))
                cum1 = jnp.broadcast_to(cum[:, h1:h1 + 1], (CHUNK, CHUNK))
                cum_p = jnp.where(lo, cum0, cum1)
                dt_p = _pair_expand(dt[:, h0:h0 + 1], dt[:, h1:h1 + 1], lo)
                xdt = x_p * dt_p
                xdt_b = xdt.astype(BF16)
                l0 = jnp.exp2(jnp.where(causal, cum0 - cum_t[h0:h0 + 1, :], -jnp.inf))
                l1 = jnp.exp2(jnp.where(causal, cum1 - cum_t[h1:h1 + 1, :], -jnp.inf))
                y = jnp.where(lo, jnp.dot((cb * l0).astype(BF16), xdt_b, preferred_element_type=F32),
                              jnp.dot((cb * l1).astype(BF16), xdt_b, preferred_element_type=F32))
                st = state_ref[:, j * LANES:(j + 1) * LANES]
                y_off = jnp.dot(c_gb, st.astype(BF16), preferred_element_type=F32)
                y = y + y_off * jnp.exp2(cum_p)
                y = y + x_p * dsk_ref[:, j * LANES:(j + 1) * LANES]
                ys.append(y)
                cum_last = cum_p[CHUNK - 1:CHUNK, :]
                xw = (xdt * jnp.exp2(cum_last - cum_p)).astype(BF16)
                s_new = jnp.dot(b_t, xw, preferred_element_type=F32)
                state_ref[:, j * LANES:(j + 1) * LANES] = st * jnp.exp2(cum_last) + s_new
        y = jnp.concatenate(ys, axis=1)
        y = y * _silu(z_ref[rows, :].astype(F32))
        y = y * lax.rsqrt(jnp.mean(jnp.square(y), axis=-1, keepdims=True) + RMS_EPS)
        o_ref[rows, :] = (y * nw_ref[...]).astype(o_ref.dtype)
        return carry

    lax.fori_loop(0, n_chunks, chunk_body, 0, unroll=2)


def _ssd(proj, dt_raw, p, bsz, seq, tt):
    m = proj.shape[0]
    per_b = seq // tt
    row_map = lambda b, t: b * per_b + t
    pad_heads = LANES - SSD_HEADS
    dtb = jnp.pad(p["ssd_dt_bias"], (0, pad_heads)).reshape(1, LANES)
    aneg = jnp.pad(-jnp.exp(p["ssd_a_log"]), (0, pad_heads)).reshape(1, LANES)
    dsk = jnp.repeat(p["ssd_d"], HEAD_DIM).reshape(1, BRANCH)
    const = lambda shape: pl.BlockSpec(shape, lambda b, t: (0, 0))
    col = lambda cidx: pl.BlockSpec((tt, BRANCH), lambda b, t: (row_map(b, t), cidx))
    return pl.pallas_call(
        functools.partial(_ssd_kernel, n_chunks=tt // CHUNK),
        out_shape=jax.ShapeDtypeStruct((m, BRANCH), BF16),
        grid=(bsz, per_b),
        in_specs=[col(COL_Z), col(COL_XS), col(COL_BC),
                  pl.BlockSpec((tt, LANES), lambda b, t: (row_map(b, t), 0)),
                  const((SSD_CONV, 2 * BRANCH)), const((1, 2 * BRANCH)),
                  const((1, LANES)), const((1, LANES)), const((1, BRANCH)), const((1, BRANCH))],
        out_specs=pl.BlockSpec((tt, BRANCH), lambda b, t: (row_map(b, t), 0)),
        scratch_shapes=[pltpu.VMEM((CHUNK + SUBLANES, 2 * BRANCH), F32),
                        pltpu.VMEM((SSD_STATE, BRANCH), F32)],
        compiler_params=_cparams(("parallel", "arbitrary")),
        name="ssd",
    )(proj, proj, proj, dt_raw, p["ssd_conv_w"], p["ssd_conv_b"].reshape(1, -1), dtb, aneg, dsk,
      p["ssd_norm_w"].reshape(1, -1))


def _rope_table_kernel(pos_ref, invf_ref, cos_ref, sin_ref):
    ang = pos_ref[...].astype(F32) * invf_ref[...]
    lane = lax.broadcasted_iota(jnp.int32, ang.shape, 1)
    first_half = (lane % HEAD_DIM) < (HEAD_DIM // 2)
    cos_ref[...] = jnp.cos(ang)
    sin = jnp.sin(ang)
    sin_ref[...] = jnp.where(first_half, -sin, sin)


def _rope_tables(positions, tt):
    half = HEAD_DIM // 2
    inv_half = ROPE_THETA ** (-jnp.arange(half, dtype=F32) / half)
    inv_freq = jnp.tile(inv_half, LANES // half).reshape(1, LANES)
    m = positions.size
    shape = jax.ShapeDtypeStruct((m, LANES), F32)
    return pl.pallas_call(
        _rope_table_kernel,
        out_shape=(shape, shape),
        grid=(m // tt,),
        in_specs=[pl.BlockSpec((tt, 1), lambda i: (i, 0)), pl.BlockSpec((1, LANES), lambda i: (0, 0))],
        out_specs=(pl.BlockSpec((tt, LANES), lambda i: (i, 0)), pl.BlockSpec((tt, LANES), lambda i: (i, 0))),
        compiler_params=_cparams(("parallel",)),
        name="rope_tables",
    )(positions.reshape(m, 1), inv_freq)


def _rope_apply(x, cos, sin_signed, first_half):
    half = HEAD_DIM // 2
    outs = []
    for cblk in range(x.shape[1] // LANES):
        xb = x[:, cblk * LANES:(cblk + 1) * LANES]
        rot = jnp.where(first_half, pltpu.roll(xb, LANES - half, 1), pltpu.roll(xb, half, 1))
        outs.append(xb * cos + rot * sin_signed)
    return outs


def _swa_kernel(q_ref, kv_ref, g_ref, cos_ref, sin_ref, sink_ref, o_ref,
                kcat_ref, vtcat_ref, qb_ref, s_ref, e_ref, inv_ref, *, n_blocks):
    tile = pl.program_id(1)

    @pl.when(tile == 0)
    def _():
        kcat_ref[...] = jnp.zeros_like(kcat_ref)
        vtcat_ref[...] = jnp.zeros_like(vtcat_ref)

    row = lax.broadcasted_iota(jnp.int32, (WINDOW, WINDOW), 0)
    lane = lax.broadcasted_iota(jnp.int32, (WINDOW, WINDOW), 1)
    lo = lane < HEAD_DIM
    top = row < HEAD_DIM
    first_half = (lane % HEAD_DIM) < (HEAD_DIM // 2)
    key = lax.broadcasted_iota(jnp.int32, (2 * WINDOW, WINDOW), 0)
    qry = lax.broadcasted_iota(jnp.int32, (2 * WINDOW, WINDOW), 1)
    band = jnp.logical_and(key > qry, key - WINDOW <= qry)
    nt = (((1,), (1,)), ((), ()))
    n_pairs = ATTN_HEADS // 2

    def block_body(bi, carry):
        rows = pl.ds(pl.multiple_of(bi * WINDOW, WINDOW), WINDOW)
        valid = jnp.logical_and(band, jnp.logical_or(key >= WINDOW, tile * n_blocks + bi > 0))
        cos = cos_ref[rows, :]
        sin_s = sin_ref[rows, :]
        q_blocks = _rope_apply(q_ref[rows, :].astype(F32) * (LOG2E * HEAD_DIM ** -0.5), cos, sin_s,
                               first_half)
        for j in range(n_pairs):
            qb_ref[j] = q_blocks[j].astype(BF16)
        kv = kv_ref[rows, :].astype(F32)
        k_blocks = _rope_apply(kv[:, 0:KV_WIDTH], cos, sin_s, first_half)

        for g in range(KV_HEADS):
            kp = k_blocks[g // 2]
            vp = kv[:, KV_WIDTH + (g // 2) * LANES:KV_WIDTH + (g // 2 + 1) * LANES]
            kr = pltpu.roll(kp, HEAD_DIM, 1)
            vr = pltpu.roll(vp, HEAD_DIM, 1)
            own_lo = (g % 2) == 0
            kcat_ref[2 * g, WINDOW:, :] = jnp.where(lo, kp if own_lo else kr, 0.0).astype(BF16)
            kcat_ref[2 * g + 1, WINDOW:, :] = jnp.where(lo, 0.0, kr if own_lo else kp).astype(BF16)
            v_t = jnp.where(lo, vp if own_lo else vr, vr if own_lo else vp).T
            vtcat_ref[2 * g, :, WINDOW:] = jnp.where(top, v_t, 0.0).astype(BF16)
            vtcat_ref[2 * g + 1, :, WINDOW:] = jnp.where(top, 0.0, v_t).astype(BF16)

        for j in range(n_pairs):
            for hh in range(2):
                s_ref[2 * j + hh] = lax.dot_general(kcat_ref[2 * (j // 2) + hh], qb_ref[j], nt,
                                                    preferred_element_type=F32)
        for h in range(ATTN_HEADS):
            s = jnp.where(valid, s_ref[h], -jnp.inf)
            sink = LOG2E * sink_ref[h:h + 1, :]
            mx = jnp.maximum(s.max(0, keepdims=True), sink)
            e = jnp.exp2(s - mx)
            inv_ref[h:h + 1, :] = 1.0 / (e.sum(0, keepdims=True) + jnp.exp2(sink - mx))
            e_ref[h] = e.astype(BF16)
        outs = []
        for j in range(n_pairs):
            g = j // 2
            acc = jnp.dot(vtcat_ref[2 * g], e_ref[2 * j], preferred_element_type=F32)
            acc = acc + jnp.dot(vtcat_ref[2 * g + 1], e_ref[2 * j + 1], preferred_element_type=F32)
            inv = jnp.where(top, inv_ref[2 * j:2 * j + 1, :], inv_ref[2 * j + 1:2 * j + 2, :])
            outs.append((acc * inv).T)
        o = jnp.concatenate(outs, axis=1)
        o_ref[rows, :] = (o * _silu(g_ref[rows, :].astype(F32))).astype(o_ref.dtype)

        for i in range(2 * KV_HEADS):
            kcat_ref[i, :WINDOW, :] = kcat_ref[i, WINDOW:, :]
            vtcat_ref[i, :, :WINDOW] = vtcat_ref[i, :, WINDOW:]
        return carry

    lax.fori_loop(0, n_blocks, block_body, 0, unroll=2)


def _swa(proj, cos_t, sin_t, sinks, bsz, seq, tt):
    m = proj.shape[0]
    per_b = seq // tt
    row_map = lambda b, t: b * per_b + t
    sinks_b = jnp.broadcast_to(sinks.reshape(ATTN_HEADS, 1), (ATTN_HEADS, LANES))
    return pl.pallas_call(
        functools.partial(_swa_kernel, n_blocks=tt // WINDOW),
        out_shape=jax.ShapeDtypeStruct((m, BRANCH), BF16),
        grid=(bsz, per_b),
        in_specs=[pl.BlockSpec((tt, BRANCH), lambda b, t: (row_map(b, t), COL_Q)),
                  pl.BlockSpec((tt, 2 * KV_WIDTH), lambda b, t: (row_map(b, t), COL_KV)),
                  pl.BlockSpec((tt, BRANCH), lambda b, t: (row_map(b, t), COL_GB)),
                  pl.BlockSpec((tt, LANES), lambda b, t: (row_map(b, t), 0)),
                  pl.BlockSpec((tt, LANES), lambda b, t: (row_map(b, t), 0)),
                  pl.BlockSpec((ATTN_HEADS, LANES), lambda b, t: (0, 0))],
        out_specs=pl.BlockSpec((tt, BRANCH), lambda b, t: (row_map(b, t), 0)),
        scratch_shapes=[pltpu.VMEM((2 * KV_HEADS, 2 * WINDOW, LANES), BF16),
                        pltpu.VMEM((2 * KV_HEADS, LANES, 2 * WINDOW), BF16),
                        pltpu.VMEM((ATTN_HEADS // 2, WINDOW, LANES), BF16),
                        pltpu.VMEM((ATTN_HEADS, 2 * WINDOW, WINDOW), F32),
                        pltpu.VMEM((ATTN_HEADS, 2 * WINDOW, WINDOW), BF16),
                        pltpu.VMEM((ATTN_HEADS, LANES), F32)],
        compiler_params=_cparams(("parallel", "arbitrary")),
        name="swa",
    )(proj, proj, proj, cos_t, sin_t, sinks_b)


def _cd_kernel(cb_ref, cc_ref, cx_ref, cg_ref, dx_ref, dg_ref, scw_ref, lcw_ref, lcb_ref, wbd_ref,
               ba_ref, bx_ref, lam_ref, yc_ref, yd_ref, cbuf_ref, dbuf_ref, h_ref, *, tt):
    pad = SUBLANES

    @pl.when(pl.program_id(1) == 0)
    def _():
        cbuf_ref[0:pad, :] = jnp.zeros((pad, BRANCH), F32)
        dbuf_ref[0:pad, :] = jnp.zeros((pad, BRANCH), F32)
        h_ref[...] = jnp.zeros_like(h_ref)

    cbuf_ref[pad:pad + tt, :] = cc_ref[...].astype(F32) * cx_ref[...].astype(F32)
    conv = scw_ref[SCONV - 1:SCONV, :] * cbuf_ref[pad:pad + tt, :]
    for k in range(SCONV - 1):
        off = pad - (SCONV - 1) + k
        conv = conv + scw_ref[k:k + 1, :] * cbuf_ref[off:off + tt, :]
    cbuf_ref[0:pad, :] = cbuf_ref[tt:tt + pad, :]
    yc_ref[...] = (cb_ref[...].astype(F32) * conv * _silu(cg_ref[...].astype(F32))).astype(yc_ref.dtype)

    dbuf_ref[pad:pad + tt, :] = dx_ref[...].astype(F32)
    xs = lcb_ref[...] + lcw_ref[LRU_CONV - 1:LRU_CONV, :] * dbuf_ref[pad:pad + tt, :]
    for k in range(LRU_CONV - 1):
        off = pad - (LRU_CONV - 1) + k
        xs = xs + lcw_ref[k:k + 1, :] * dbuf_ref[off:off + tt, :]
    dbuf_ref[0:pad, :] = dbuf_ref[tt:tt + pad, :]
    xs_b = xs.astype(BF16)
    gates = [jnp.dot(xs_b[:, q * LRU_GROUP:(q + 1) * LRU_GROUP], wbd_ref[q], preferred_element_type=F32)
             for q in range(BRANCH // LRU_GROUP)]
    pre_a = jnp.concatenate([gq[:, 0:LRU_GROUP] for gq in gates], axis=1)
    pre_x = jnp.concatenate([gq[:, LRU_GROUP:2 * LRU_GROUP] for gq in gates], axis=1)
    rg = jax.nn.sigmoid(pre_a + ba_ref[...])
    ig = jax.nn.sigmoid(pre_x + bx_ref[...])
    log_a = -LRU_C * rg * _softplus(-lam_ref[...])
    a = jnp.exp(log_a)
    u = jnp.sqrt(jnp.tanh(-log_a) * (a * a + 1.0)) * (ig * xs)

    groups = tt // SUBLANES
    a3 = a.reshape(groups, SUBLANES, BRANCH)
    u3 = u.reshape(groups, SUBLANES, BRANCH)
    sub = lax.broadcasted_iota(jnp.int32, a3.shape, 1)
    k = 1
    while k < SUBLANES:
        keep = sub >= k
        a_s = jnp.where(keep, pltpu.roll(a3, k, 1), 1.0)
        u_s = jnp.where(keep, pltpu.roll(u3, k, 1), 0.0)
        u3 = a3 * u_s + u3
        a3 = a3 * a_s
        k *= 2
    carry = h_ref[0:1, :]
    hs = []
    for gi in range(groups):
        hg = u3[gi] + a3[gi] * carry
        carry = hg[SUBLANES - 1:SUBLANES, :]
        hs.append(hg)
    h_ref[...] = jnp.broadcast_to(carry, h_ref.shape)
    h = jnp.concatenate(hs, axis=0)
    yd_ref[...] = (h * _silu(dg_ref[...].astype(F32))).astype(yd_ref.dtype)


def _block_diag_pairs(w_a, w_x):
    per = LRU_GROUP // HEAD_DIM
    eye = jnp.eye(per, dtype=w_a.dtype)

    def bd(w):
        w4 = w.reshape(LRU_BLOCKS // per, per, HEAD_DIM, HEAD_DIM)
        return jnp.einsum('qiab,ij->qiajb', w4, eye).reshape(LRU_BLOCKS // per, LRU_GROUP, LRU_GROUP)

    return jnp.concatenate([bd(w_a), bd(w_x)], axis=2).astype(BF16)


def _cd(proj, p, bsz, seq, tt):
    m = proj.shape[0]
    per_b = seq // tt
    row_map = lambda b, t: b * per_b + t
    col = lambda cidx: pl.BlockSpec((tt, BRANCH), lambda b, t: (row_map(b, t), cidx))
    const = lambda shape: pl.BlockSpec(shape, lambda b, t: (0,) * len(shape))
    wbd = _block_diag_pairs(p["lru_w_a"], p["lru_w_x"])
    out_spec = pl.BlockSpec((tt, BRANCH), lambda b, t: (row_map(b, t), 0))
    vec = lambda a: a.reshape(1, -1)
    return pl.pallas_call(
        functools.partial(_cd_kernel, tt=tt),
        out_shape=(jax.ShapeDtypeStruct((m, BRANCH), BF16), jax.ShapeDtypeStruct((m, BRANCH), BF16)),
        grid=(bsz, per_b),
        in_specs=[col(COL_CB), col(COL_CC), col(COL_CX), col(COL_CG), col(COL_DX), col(COL_DG),
                  const((SCONV, BRANCH)), const((LRU_CONV, BRANCH)), const((1, BRANCH)),
                  const(wbd.shape), const((1, BRANCH)), const((1, BRANCH)), const((1, BRANCH))],
        out_specs=(out_spec, out_spec),
        scratch_shapes=[pltpu.VMEM((tt + SUBLANES, BRANCH), F32),
                        pltpu.VMEM((tt + SUBLANES, BRANCH), F32),
                        pltpu.VMEM((SUBLANES, BRANCH), F32)],
        compiler_params=_cparams(("parallel", "arbitrary")),
        name="sconv_rglru",
    )(proj, proj, proj, proj, proj, proj, p["sconv_w"], p["lru_conv_w"], vec(p["lru_conv_b"]), wbd,
      vec(p["lru_b_a"]), vec(p["lru_b_x"]), vec(p["lru_lambda"]))


def _merge_kernel(ya_ref, yb_ref, yc_ref, yd_ref, lg_ref, x_ref, gate_ref, bg_ref, wb_ref, wo_ref,
                  lnw_ref, lnb_ref, o_ref):
    m = None
    for k, y_ref in enumerate((ya_ref, yb_ref, yc_ref, yd_ref)):
        gk = jax.nn.sigmoid(lg_ref[:, k * D_MODEL:(k + 1) * D_MODEL].astype(F32) + bg_ref[k:k + 1, :])
        t = gk * jnp.dot(y_ref[...], wb_ref[k], preferred_element_type=F32)
        m = t if m is None else m + t
    out = jnp.dot(m.astype(BF16), wo_ref[...], preferred_element_type=F32)
    r = ALPHA * x_ref[...] + gate_ref[0] * out
    mu = jnp.mean(r, axis=-1, keepdims=True)
    rc = r - mu
    var = jnp.mean(jnp.square(rc), axis=-1, keepdims=True)
    o_ref[...] = rc * lax.rsqrt(var + LN_EPS) * lnw_ref[...] + lnb_ref[...]


def _merge(ya, yb, yc, yd, proj, x2, ada3, b_gate, w_branch, w_out, ln_w, ln_b, layer, seq, tm):
    m, d = x2.shape
    per_b = seq // tm
    ycol = pl.BlockSpec((tm, BRANCH), lambda i: (i, 0))
    once = pl.Buffered(1)
    return pl.pallas_call(
        _merge_kernel,
        out_shape=jax.ShapeDtypeStruct((m, d), F32),
        grid=(m // tm,),
        in_specs=[ycol, ycol, ycol, ycol,
                  pl.BlockSpec((tm, N_BRANCH * d), lambda i: (i, 0)),
                  pl.BlockSpec((tm, d), lambda i: (i, 0)),
                  pl.BlockSpec((1, 1, d), lambda i: (i // per_b, 0, 2)),
                  pl.BlockSpec((N_BRANCH, d), lambda i: (0, 0)),
                  pl.BlockSpec((None, N_BRANCH, BRANCH, d), lambda i: (layer, 0, 0, 0), pipeline_mode=once),
                  pl.BlockSpec((None, d, d), lambda i: (layer, 0, 0), pipeline_mode=once),
                  pl.BlockSpec((1, d), lambda i: (0, 0)),
                  pl.BlockSpec((1, d), lambda i: (0, 0))],
        out_specs=pl.BlockSpec((tm, d), lambda i: (i, 0)),
        compiler_params=_cparams(("parallel",)),
        name="merge",
    )(ya, yb, yc, yd, proj, x2, ada3, b_gate, w_branch, w_out, ln_w.reshape(1, -1), ln_b.reshape(1, -1))


def kernel(x, c, positions, w_ada, b_ada, w_in, b_gate, ssd_conv_w, ssd_conv_b, ssd_dt_bias, ssd_a_log,
           ssd_d, ssd_norm_w, attn_sinks, sconv_w, lru_conv_w, lru_conv_b, lru_w_a, lru_b_a, lru_w_x,
           lru_b_x, lru_lambda, w_branch, w_out, ln_w, ln_b):
    bsz, seq, d = x.shape
    small = dict(b_gate=b_gate, ssd_conv_w=ssd_conv_w, ssd_conv_b=ssd_conv_b, ssd_dt_bias=ssd_dt_bias,
                 ssd_a_log=ssd_a_log, ssd_d=ssd_d, ssd_norm_w=ssd_norm_w, attn_sinks=attn_sinks,
                 sconv_w=sconv_w, lru_conv_w=lru_conv_w, lru_conv_b=lru_conv_b, lru_w_a=lru_w_a,
                 lru_b_a=lru_b_a, lru_w_x=lru_w_x, lru_b_x=lru_b_x, lru_lambda=lru_lambda,
                 ln_w=ln_w, ln_b=ln_b)
    w_main, w_dt = _rearrange_w_in(w_in)
    w_branch_b = w_branch.astype(BF16)
    w_out_b = w_out.astype(BF16)
    cos_t, sin_t = _rope_tables(positions, tt=min(2048, bsz * seq))
    x2 = x.reshape(bsz * seq, d)
    for layer in range(DEPTH):
        p = {k: v[layer] for k, v in small.items()}
        ada3 = _ada(c, w_ada, b_ada, layer).reshape(bsz, 1, 3 * d)
        proj, dt_raw = _inproj(x2, ada3, w_main, w_dt, layer, seq, tm=min(1024, seq), tn=1536)
        ya = _ssd(proj, dt_raw, p, bsz, seq, tt=min(512, seq))
        yb = _swa(proj, cos_t, sin_t, p["attn_sinks"], bsz, seq, tt=min(512, seq))
        yc, yd = _cd(proj, p, bsz, seq, tt=min(256, seq))
        x2 = _merge(ya, yb, yc, yd, proj, x2, ada3, p["b_gate"], w_branch_b, w_out_b, p["ln_w"], p["ln_b"],
                    layer, seq, tm=min(256, seq))
    return x2.reshape(bsz, seq, d)
```

```python
import functools

import jax
import jax.numpy as jnp
from jax import lax
from jax.experimental import pallas as pl
from jax.experimental.pallas import tpu as pltpu

F32 = jnp.float32
BF16 = jnp.bfloat16

LANES = 128
SUBLANES = 8
VMEM_LIMIT = 56 * 1024 * 1024

D_MODEL = 2048
BRANCH = D_MODEL // 2
N_BRANCH = 4
HEAD_DIM = 64
SSD_HEADS = BRANCH // HEAD_DIM
SSD_GROUPS = 4
SSD_STATE = 128
SSD_CONV = 4
CHUNK = 128
ATTN_HEADS = BRANCH // HEAD_DIM
KV_HEADS = 4
KV_WIDTH = KV_HEADS * HEAD_DIM
WINDOW = 128
ROPE_THETA = 10000.0
SCONV = 3
LRU_BLOCKS = 16
LRU_CONV = 4
LRU_C = 8.0
LRU_GROUP = 256
LN_EPS = 1e-5
RMS_EPS = 1e-5
DEPTH = 2
ALPHA = (2.0 * DEPTH) ** 0.25
LOG2E = 1.4426950408889634

MERGE_COLS = N_BRANCH * D_MODEL
_C0 = MERGE_COLS // BRANCH
COL_Z, COL_XS, COL_BC, COL_Q, COL_GB = _C0, _C0 + 1, _C0 + 2, _C0 + 3, _C0 + 4
COL_CB, COL_CC, COL_CX, COL_CG, COL_DX, COL_DG = (_C0 + 5, _C0 + 6, _C0 + 7, _C0 + 8, _C0 + 9, _C0 + 10)
COL_KV = (MERGE_COLS + 11 * BRANCH) // (2 * KV_WIDTH)
ORIG_DT = 3 * BRANCH
ORIG_Q = ORIG_DT + SSD_HEADS
ORIG_K = ORIG_Q + BRANCH
ORIG_GB = ORIG_K + 2 * KV_WIDTH
ORIG_MERGE = ORIG_GB + 7 * BRANCH


IN_COLS = ORIG_MERGE + MERGE_COLS
MAIN_COLS = IN_COLS - SSD_HEADS
_W_PIECES = ((ORIG_MERGE, MERGE_COLS), (0, ORIG_DT), (ORIG_Q, BRANCH), (ORIG_GB, 7 * BRANCH),
             (ORIG_K, 2 * KV_WIDTH))


PREP_COLS = 512


def _rearrange_kernel(wt_ref, main_ref):
    main_ref[...] = wt_ref[...].T.astype(main_ref.dtype)


COL_UNIT = 16


def _source_column_units(j):
    dst = j * (PREP_COLS // COL_UNIT)
    src, off = None, 0
    for start, width in _W_PIECES:
        cand = dst - off // COL_UNIT + start // COL_UNIT
        src = cand if src is None else jnp.where(dst >= off // COL_UNIT, cand, src)
        off += width
    return src


def _rearrange_w_in(w_in):
    depth, d, n = w_in.shape
    w_main = pl.pallas_call(
        _rearrange_kernel,
        out_shape=jax.ShapeDtypeStruct((depth, d, MAIN_COLS), BF16),
        grid=(depth, MAIN_COLS // PREP_COLS),
        in_specs=[pl.BlockSpec((pl.Element(PREP_COLS), pl.Element(d)),
                               lambda l, j: ((l * (n // COL_UNIT) + _source_column_units(j)) * COL_UNIT, 0))],
        out_specs=pl.BlockSpec((None, d, PREP_COLS), lambda l, j: (l, 0, j)),
        compiler_params=_cparams(("parallel", "parallel")),
        name="w_in_layout",
    )(jnp.swapaxes(w_in, 1, 2).reshape(depth * n, d))
    w_dt = jnp.pad(w_in[..., ORIG_DT:ORIG_Q], ((0, 0), (0, 0), (0, LANES - SSD_HEADS))).astype(BF16)
    return w_main, w_dt


def _softplus(x):
    return jnp.maximum(x, 0.0) + jnp.log1p(jnp.exp(-jnp.abs(x)))


def _silu(x):
    return x * jax.nn.sigmoid(x)


def _cparams(sem):
    return pltpu.CompilerParams(dimension_semantics=sem, vmem_limit_bytes=VMEM_LIMIT)


def _ada_kernel(c_ref, w_ref, b_ref, o_ref):
    ca = _silu(c_ref[...])
    o_ref[...] = jnp.dot(ca, w_ref[...], preferred_element_type=F32,
                         precision=lax.Precision.HIGHEST) + b_ref[...]


def _ada(c, w_ada, b_ada, layer):
    bsz, d = c.shape
    n = w_ada.shape[-1]
    tn = 768
    return pl.pallas_call(
        _ada_kernel,
        out_shape=jax.ShapeDtypeStruct((bsz, n), F32),
        grid=(n // tn,),
        in_specs=[pl.BlockSpec((bsz, d), lambda j: (0, 0)),
                  pl.BlockSpec((None, d, tn), lambda j: (layer, 0, j)),
                  pl.BlockSpec((None, 1, tn), lambda j: (layer, 0, j))],
        out_specs=pl.BlockSpec((bsz, tn), lambda j: (0, j)),
        compiler_params=_cparams(("arbitrary",)),
        name="ada",
    )(c, w_ada, b_ada.reshape(b_ada.shape[0], 1, n))


def _inproj_kernel(x_ref, shift_ref, scale_ref, w_ref, wdt_ref, o_ref, dt_ref, h_ref):
    @pl.when(pl.program_id(1) == 0)
    def _():
        h = x_ref[...] * (1.0 + scale_ref[0]) + shift_ref[0]
        hb = h.astype(BF16)
        h_ref[...] = hb
        dt_ref[...] = jnp.dot(hb, wdt_ref[...], preferred_element_type=F32)

    o_ref[...] = jnp.dot(h_ref[...], w_ref[...], preferred_element_type=F32).astype(o_ref.dtype)


def _inproj(x2, ada3, w_main, w_dt, layer, seq, tm, tn):
    m, d = x2.shape
    n = w_main.shape[-1]
    per_b = seq // tm
    return pl.pallas_call(
        _inproj_kernel,
        out_shape=(jax.ShapeDtypeStruct((m, n), BF16), jax.ShapeDtypeStruct((m, LANES), F32)),
        grid=(m // tm, n // tn),
        in_specs=[pl.BlockSpec((tm, d), lambda i, j: (i, 0)),
                  pl.BlockSpec((1, 1, d), lambda i, j: (i // per_b, 0, 0)),
                  pl.BlockSpec((1, 1, d), lambda i, j: (i // per_b, 0, 1)),
                  pl.BlockSpec((None, d, tn), lambda i, j: (layer, 0, j)),
                  pl.BlockSpec((None, d, LANES), lambda i, j: (layer, 0, 0))],
        out_specs=(pl.BlockSpec((tm, tn), lambda i, j: (i, j)),
                   pl.BlockSpec((tm, LANES), lambda i, j: (i, 0))),
        scratch_shapes=[pltpu.VMEM((tm, d), BF16)],
        compiler_params=_cparams(("parallel", "arbitrary")),
        name="inproj",
    )(x2, ada3, ada3, w_main, w_dt)


def _pair_expand(col_a, col_b, lo):
    shape = (col_a.shape[0], LANES)
    return jnp.where(lo, jnp.broadcast_to(col_a, shape), jnp.broadcast_to(col_b, shape))


def _ssd_setup(z_ref, xs_ref, bc_ref, dt_ref, cw_ref, cb_ref, dtb_ref, aneg_ref, dsk_ref, nw_ref,
               o_ref, xbuf_ref, state_ref):
    pad = SUBLANES

    @pl.when(pl.program_id(1) == 0)
    def _():
        xbuf_ref[0:pad, :] = jnp.zeros((pad, 2 * BRANCH), F32)
        state_ref[...] = jnp.zeros_like(state_ref)

    row = lax.broadcasted_iota(jnp.int32, (CHUNK, CHUNK), 0)
    col = lax.broadcasted_iota(jnp.int32, (CHUNK, CHUNK), 1)
    causal = row >= col
    tril = causal.astype(F32)
    lo = col < HEAD_DIM

    def chunk_body(ci, carry):
        rows = pl.ds(pl.multiple_of(ci * CHUNK, CHUNK), CHUNK)
        xbuf_ref[pad:pad + CHUNK, 0:BRANCH] = xs_ref[rows, :].astype(F32)
        xbuf_ref[pad:pad + CHUNK, BRANCH:2 * BRANCH] = bc_ref[rows, :].astype(F32)
        conv = cb_ref[...] + cw_ref[SSD_CONV - 1:SSD_CONV, :] * xbuf_ref[pad:pad + CHUNK, :]
        for k in range(SSD_CONV - 1):
            off = pad - (SSD_CONV - 1) + k
            conv = conv + cw_ref[k:k + 1, :] * xbuf_ref[off:off + CHUNK, :]
        xbuf_ref[0:pad, :] = xbuf_ref[CHUNK:CHUNK + pad, :]
        xbc = _silu(conv)
        xc = xbc[:, 0:BRANCH]

        dt = _softplus(dt_ref[rows, :] + dtb_ref[...])
        da = dt * aneg_ref[...]
        cum = LOG2E * jnp.dot(tril, da, preferred_element_type=F32, precision=lax.Precision.HIGHEST)
        cum_t = cum.T

        ys = []
        for g in range(SSD_GROUPS):
            b_g = xbc[:, BRANCH + g * SSD_STATE:BRANCH + (g + 1) * SSD_STATE]
            c_g = xbc[:, BRANCH + (SSD_GROUPS + g) * SSD_STATE:BRANCH + (SSD_GROUPS + g + 1) * SSD_STATE]
            b_gb = b_g.astype(BF16)
            c_gb = c_g.astype(BF16)
            cb = lax.dot_general(c_gb, b_gb, (((1,), (1,)), ((), ())), preferred_element_type=F32)
            b_t = b_g.T.astype(BF16)
            for jp in range(2):
                j = 2 * g + jp
                h0, h1 = 2 * j, 2 * j + 1
                x_p = xc[:, j * LANES:(j + 1) * LANES]
                cum0 = jnp.broadcast_to(cum[:, h0:h0 + 1], (CHUNK, CHUNK))
                cum1 = jnp.broadcast_to(cum[:, h1:h1 + 1], (CHUNK, CHUNK))
                cum_p = jnp.where(lo, cum0, cum1)
                dt_p = _pair_expand(dt[:, h0:h0 + 1], dt[:, h1:h1 + 1], lo)
                xdt = x_p * dt_p
                xdt_b = xdt.astype(BF16)
                l0 = jnp.exp2(jnp.where(causal, cum0 - cum_t[h0:h0 + 1, :], -jnp.inf))
                l1 = jnp.exp2(jnp.where(causal, cum1 - cum_t[h1:h1 + 1, :], -jnp.inf))
                y = jnp.where(lo, jnp.dot((cb * l0).astype(BF16), xdt_b, preferred_element_type=F32),
                              jnp.dot((cb * l1).astype(BF16), xdt_b, preferred_element_type=F32))
                st = state_ref[:, j * LANES:(j + 1) * LANES]
                y_off = jnp.dot(c_gb, st.astype(BF16), preferred_element_type=F32)
                y = y + y_off * jnp.exp2(cum_p)
                y = y + x_p * dsk_ref[:, j * LANES:(j + 1) * LANES]
                ys.append(y)
                cum_last = cum_p[CHUNK - 1:CHUNK, :]
                xw = (xdt * jnp.exp2(cum_last - cum_p)).astype(BF16)
                s_new = jnp.dot(b_t, xw, preferred_element_type=F32)
                state_ref[:, j * LANES:(j + 1) * LANES] = st * jnp.exp2(cum_last) + s_new
        y = jnp.concatenate(ys, axis=1)
        y = y * _silu(z_ref[rows, :].astype(F32))
        y = y * lax.rsqrt(jnp.mean(jnp.square(y), axis=-1, keepdims=True) + RMS_EPS)
        o_ref[rows, :] = (y * nw_ref[...]).astype(o_ref.dtype)
        return carry

    return chunk_body


def _ssd_operands(proj, dt_raw, p, tt, row_map):
    pad_heads = LANES - SSD_HEADS
    dtb = jnp.pad(p["ssd_dt_bias"], (0, pad_heads)).reshape(1, LANES)
    aneg = jnp.pad(-jnp.exp(p["ssd_a_log"]), (0, pad_heads)).reshape(1, LANES)
    dsk = jnp.repeat(p["ssd_d"], HEAD_DIM).reshape(1, BRANCH)
    const = lambda shape: pl.BlockSpec(shape, lambda b, t: (0, 0))
    col = lambda cidx: pl.BlockSpec((tt, BRANCH), lambda b, t: (row_map(b, t), cidx))
    in_specs = [col(COL_Z), col(COL_XS), col(COL_BC),
                pl.BlockSpec((tt, LANES), lambda b, t: (row_map(b, t), 0)),
                const((SSD_CONV, 2 * BRANCH)), const((1, 2 * BRANCH)),
                const((1, LANES)), const((1, LANES)), const((1, BRANCH)), const((1, BRANCH))]
    operands = [proj, proj, proj, dt_raw, p["ssd_conv_w"], p["ssd_conv_b"].reshape(1, -1), dtb, aneg, dsk,
                p["ssd_norm_w"].reshape(1, -1)]
    scratch = [pltpu.VMEM((CHUNK + SUBLANES, 2 * BRANCH), F32), pltpu.VMEM((SSD_STATE, BRANCH), F32)]
    return in_specs, operands, scratch


def _rope_table_kernel(pos_ref, invf_ref, cos_ref, sin_ref):
    ang = pos_ref[...].astype(F32) * invf_ref[...]
    lane = lax.broadcasted_iota(jnp.int32, ang.shape, 1)
    first_half = (lane % HEAD_DIM) < (HEAD_DIM // 2)
    cos_ref[...] = jnp.cos(ang)
    sin = jnp.sin(ang)
    sin_ref[...] = jnp.where(first_half, -sin, sin)


def _rope_tables(positions, tt):
    half = HEAD_DIM // 2
    inv_half = ROPE_THETA ** (-jnp.arange(half, dtype=F32) / half)
    inv_freq = jnp.tile(inv_half, LANES // half).reshape(1, LANES)
    m = positions.size
    shape = jax.ShapeDtypeStruct((m, LANES), F32)
    return pl.pallas_call(
        _rope_table_kernel,
        out_shape=(shape, shape),
        grid=(m // tt,),
        in_specs=[pl.BlockSpec((tt, 1), lambda i: (i, 0)), pl.BlockSpec((1, LANES), lambda i: (0, 0))],
        out_specs=(pl.BlockSpec((tt, LANES), lambda i: (i, 0)), pl.BlockSpec((tt, LANES), lambda i: (i, 0))),
        compiler_params=_cparams(("parallel",)),
        name="rope_tables",
    )(positions.reshape(m, 1), inv_freq)


def _rope_apply(x, cos, sin_signed, first_half):
    half = HEAD_DIM // 2
    outs = []
    for cblk in range(x.shape[1] // LANES):
        xb = x[:, cblk * LANES:(cblk + 1) * LANES]
        rot = jnp.where(first_half, pltpu.roll(xb, LANES - half, 1), pltpu.roll(xb, half, 1))
        outs.append(xb * cos + rot * sin_signed)
    return outs


def _swa_setup(q_ref, kv_ref, g_ref, cos_ref, sin_ref, sink_ref, o_ref,
               kcat_ref, vtcat_ref, qb_ref, s_ref, e_ref, inv_ref, n_blocks):
    tile = pl.program_id(1)

    @pl.when(tile == 0)
    def _():
        kcat_ref[...] = jnp.zeros_like(kcat_ref)
        vtcat_ref[...] = jnp.zeros_like(vtcat_ref)

    row = lax.broadcasted_iota(jnp.int32, (WINDOW, WINDOW), 0)
    lane = lax.broadcasted_iota(jnp.int32, (WINDOW, WINDOW), 1)
    lo = lane < HEAD_DIM
    top = row < HEAD_DIM
    first_half = (lane % HEAD_DIM) < (HEAD_DIM // 2)
    key = lax.broadcasted_iota(jnp.int32, (2 * WINDOW, WINDOW), 0)
    qry = lax.broadcasted_iota(jnp.int32, (2 * WINDOW, WINDOW), 1)
    band = jnp.logical_and(key > qry, key - WINDOW <= qry)
    nt = (((1,), (1,)), ((), ()))
    n_pairs = ATTN_HEADS // 2

    def block_body(bi, carry):
        rows = pl.ds(pl.multiple_of(bi * WINDOW, WINDOW), WINDOW)
        valid = jnp.logical_and(band, jnp.logical_or(key >= WINDOW, tile * n_blocks + bi > 0))
        cos = cos_ref[rows, :]
        sin_s = sin_ref[rows, :]
        q_blocks = _rope_apply(q_ref[rows, :].astype(F32) * (LOG2E * HEAD_DIM ** -0.5), cos, sin_s,
                               first_half)
        for j in range(n_pairs):
            qb_ref[j] = q_blocks[j].astype(BF16)
        kv = kv_ref[rows, :].astype(F32)
        k_blocks = _rope_apply(kv[:, 0:KV_WIDTH], cos, sin_s, first_half)

        for g in range(KV_HEADS):
            kp = k_blocks[g // 2]
            vp = kv[:, KV_WIDTH + (g // 2) * LANES:KV_WIDTH + (g // 2 + 1) * LANES]
            kr = pltpu.roll(kp, HEAD_DIM, 1)
            vr = pltpu.roll(vp, HEAD_DIM, 1)
            own_lo = (g % 2) == 0
            kcat_ref[2 * g, WINDOW:, :] = jnp.where(lo, kp if own_lo else kr, 0.0).astype(BF16)
            kcat_ref[2 * g + 1, WINDOW:, :] = jnp.where(lo, 0.0, kr if own_lo else kp).astype(BF16)
            v_t = jnp.where(lo, vp if own_lo else vr, vr if own_lo else vp).T
            vtcat_ref[2 * g, :, WINDOW:] = jnp.where(top, v_t, 0.0).astype(BF16)
            vtcat_ref[2 * g + 1, :, WINDOW:] = jnp.where(top, 0.0, v_t).astype(BF16)

        for j in range(n_pairs):
            for hh in range(2):
                s_ref[2 * j + hh] = lax.dot_general(kcat_ref[2 * (j // 2) + hh], qb_ref[j], nt,
                                                    preferred_element_type=F32)
        for h in range(ATTN_HEADS):
            s = jnp.where(valid, s_ref[h], -jnp.inf)
            sink = LOG2E * sink_ref[h:h + 1, :]
            mx = jnp.maximum(s.max(0, keepdims=True), sink)
            e = jnp.exp2(s - mx)
            inv_ref[h:h + 1, :] = 1.0 / (e.sum(0, keepdims=True) + jnp.exp2(sink - mx))
            e_ref[h] = e.astype(BF16)
        outs = []
        for j in range(n_pairs):
            g = j // 2
            acc = jnp.dot(vtcat_ref[2 * g], e_ref[2 * j], preferred_element_type=F32)
            acc = acc + jnp.dot(vtcat_ref[2 * g + 1], e_ref[2 * j + 1], preferred_element_type=F32)
            inv = jnp.where(top, inv_ref[2 * j:2 * j + 1, :], inv_ref[2 * j + 1:2 * j + 2, :])
            outs.append((acc * inv).T)
        o = jnp.concatenate(outs, axis=1)
        o_ref[rows, :] = (o * _silu(g_ref[rows, :].astype(F32))).astype(o_ref.dtype)

        for i in range(2 * KV_HEADS):
            kcat_ref[i, :WINDOW, :] = kcat_ref[i, WINDOW:, :]
            vtcat_ref[i, :, :WINDOW] = vtcat_ref[i, :, WINDOW:]
        return carry

    return block_body


N_SSD_IN, N_SWA_IN, N_SSD_SCRATCH = 10, 6, 2


def _ssd_swa_kernel(*refs, n_units):
    ssd_in = refs[:N_SSD_IN]
    swa_in = refs[N_SSD_IN:N_SSD_IN + N_SWA_IN]
    oa_ref, ob_ref = refs[N_SSD_IN + N_SWA_IN:N_SSD_IN + N_SWA_IN + 2]
    scratch = refs[N_SSD_IN + N_SWA_IN + 2:]
    chunk_body = _ssd_setup(*ssd_in, oa_ref, *scratch[:N_SSD_SCRATCH])
    block_body = _swa_setup(*swa_in, ob_ref, *scratch[N_SSD_SCRATCH:], n_units)

    def unit_body(i, carry):
        chunk_body(i, carry)
        block_body(i, carry)
        return carry

    lax.fori_loop(0, n_units, unit_body, 0, unroll=2)


def _ssd_swa(proj, dt_raw, cos_t, sin_t, p, bsz, seq, tt):
    m = proj.shape[0]
    per_b = seq // tt
    row_map = lambda b, t: b * per_b + t
    ssd_specs, ssd_ops, ssd_scratch = _ssd_operands(proj, dt_raw, p, tt, row_map)
    sinks_b = jnp.broadcast_to(p["attn_sinks"].reshape(ATTN_HEADS, 1), (ATTN_HEADS, LANES))
    swa_specs = [pl.BlockSpec((tt, BRANCH), lambda b, t: (row_map(b, t), COL_Q)),
                 pl.BlockSpec((tt, 2 * KV_WIDTH), lambda b, t: (row_map(b, t), COL_KV)),
                 pl.BlockSpec((tt, BRANCH), lambda b, t: (row_map(b, t), COL_GB)),
                 pl.BlockSpec((tt, LANES), lambda b, t: (row_map(b, t), 0)),
                 pl.BlockSpec((tt, LANES), lambda b, t: (row_map(b, t), 0)),
                 pl.BlockSpec((ATTN_HEADS, LANES), lambda b, t: (0, 0))]
    swa_scratch = [pltpu.VMEM((2 * KV_HEADS, 2 * WINDOW, LANES), BF16),
                   pltpu.VMEM((2 * KV_HEADS, LANES, 2 * WINDOW), BF16),
                   pltpu.VMEM((ATTN_HEADS // 2, WINDOW, LANES), BF16),
                   pltpu.VMEM((ATTN_HEADS, 2 * WINDOW, WINDOW), F32),
                   pltpu.VMEM((ATTN_HEADS, 2 * WINDOW, WINDOW), BF16),
                   pltpu.VMEM((ATTN_HEADS, LANES), F32)]
    assert len(ssd_specs) == N_SSD_IN and len(swa_specs) == N_SWA_IN and len(ssd_scratch) == N_SSD_SCRATCH
    out_spec = pl.BlockSpec((tt, BRANCH), lambda b, t: (row_map(b, t), 0))
    out_shape = jax.ShapeDtypeStruct((m, BRANCH), BF16)
    return pl.pallas_call(
        functools.partial(_ssd_swa_kernel, n_units=tt // CHUNK),
        out_shape=(out_shape, out_shape),
        grid=(bsz, per_b),
        in_specs=ssd_specs + swa_specs,
        out_specs=(out_spec, out_spec),
        scratch_shapes=ssd_scratch + swa_scratch,
        compiler_params=_cparams(("parallel", "arbitrary")),
        name="ssd_swa",
    )(*ssd_ops, proj, proj, proj, cos_t, sin_t, sinks_b)


def _cd_kernel(cb_ref, cc_ref, cx_ref, cg_ref, dx_ref, dg_ref, scw_ref, lcw_ref, lcb_ref, wbd_ref,
               ba_ref, bx_ref, lam_ref, yc_ref, yd_ref, cbuf_ref, dbuf_ref, h_ref, *, tt):
    pad = SUBLANES

    @pl.when(pl.program_id(1) == 0)
    def _():
        cbuf_ref[0:pad, :] = jnp.zeros((pad, BRANCH), F32)
        dbuf_ref[0:pad, :] = jnp.zeros((pad, BRANCH), F32)
        h_ref[...] = jnp.zeros_like(h_ref)

    cbuf_ref[pad:pad + tt, :] = cc_ref[...].astype(F32) * cx_ref[...].astype(F32)
    conv = scw_ref[SCONV - 1:SCONV, :] * cbuf_ref[pad:pad + tt, :]
    for k in range(SCONV - 1):
        off = pad - (SCONV - 1) + k
        conv = conv + scw_ref[k:k + 1, :] * cbuf_ref[off:off + tt, :]
    cbuf_ref[0:pad, :] = cbuf_ref[tt:tt + pad, :]
    yc_ref[...] = (cb_ref[...].astype(F32) * conv * _silu(cg_ref[...].astype(F32))).astype(yc_ref.dtype)

    dbuf_ref[pad:pad + tt, :] = dx_ref[...].astype(F32)
    xs = lcb_ref[...] + lcw_ref[LRU_CONV - 1:LRU_CONV, :] * dbuf_ref[pad:pad + tt, :]
    for k in range(LRU_CONV - 1):
        off = pad - (LRU_CONV - 1) + k
        xs = xs + lcw_ref[k:k + 1, :] * dbuf_ref[off:off + tt, :]
    dbuf_ref[0:pad, :] = dbuf_ref[tt:tt + pad, :]
    xs_b = xs.astype(BF16)
    gates = [jnp.dot(xs_b[:, q * LRU_GROUP:(q + 1) * LRU_GROUP], wbd_ref[q], preferred_element_type=F32)
             for q in range(BRANCH // LRU_GROUP)]
    pre_a = jnp.concatenate([gq[:, 0:LRU_GROUP] for gq in gates], axis=1)
    pre_x = jnp.concatenate([gq[:, LRU_GROUP:2 * LRU_GROUP] for gq in gates], axis=1)
    rg = jax.nn.sigmoid(pre_a + ba_ref[...])
    ig = jax.nn.sigmoid(pre_x + bx_ref[...])
    log_a = -LRU_C * rg * _softplus(-lam_ref[...])
    a = jnp.exp(log_a)
    u = jnp.sqrt(jnp.tanh(-log_a) * (a * a + 1.0)) * (ig * xs)

    groups = tt // SUBLANES
    a3 = a.reshape(groups, SUBLANES, BRANCH)
    u3 = u.reshape(groups, SUBLANES, BRANCH)
    sub = lax.broadcasted_iota(jnp.int32, a3.shape, 1)
    k = 1
    while k < SUBLANES:
        keep = sub >= k
        a_s = jnp.where(keep, pltpu.roll(a3, k, 1), 1.0)
        u_s = jnp.where(keep, pltpu.roll(u3, k, 1), 0.0)
        u3 = a3 * u_s + u3
        a3 = a3 * a_s
        k *= 2
    carry = h_ref[0:1, :]
    hs = []
    for gi in range(groups):
        hg = u3[gi] + a3[gi] * carry
        carry = hg[SUBLANES - 1:SUBLANES, :]
        hs.append(hg)
    h_ref[...] = jnp.broadcast_to(carry, h_ref.shape)
    h = jnp.concatenate(hs, axis=0)
    yd_ref[...] = (h * _silu(dg_ref[...].astype(F32))).astype(yd_ref.dtype)


def _block_diag_pairs(w_a, w_x):
    per = LRU_GROUP // HEAD_DIM
    eye = jnp.eye(per, dtype=w_a.dtype)

    def bd(w):
        w4 = w.reshape(LRU_BLOCKS // per, per, HEAD_DIM, HEAD_DIM)
        return jnp.einsum('qiab,ij->qiajb', w4, eye).reshape(LRU_BLOCKS // per, LRU_GROUP, LRU_GROUP)

    return jnp.concatenate([bd(w_a), bd(w_x)], axis=2).astype(BF16)


def _cd(proj, p, bsz, seq, tt):
    m = proj.shape[0]
    per_b = seq // tt
    row_map = lambda b, t: b * per_b + t
    col = lambda cidx: pl.BlockSpec((tt, BRANCH), lambda b, t: (row_map(b, t), cidx))
    const = lambda shape: pl.BlockSpec(shape, lambda b, t: (0,) * len(shape))
    wbd = _block_diag_pairs(p["lru_w_a"], p["lru_w_x"])
    out_spec = pl.BlockSpec((tt, BRANCH), lambda b, t: (row_map(b, t), 0))
    vec = lambda a: a.reshape(1, -1)
    return pl.pallas_call(
        functools.partial(_cd_kernel, tt=tt),
        out_shape=(jax.ShapeDtypeStruct((m, BRANCH), BF16), jax.ShapeDtypeStruct((m, BRANCH), BF16)),
        grid=(bsz, per_b),
        in_specs=[col(COL_CB), col(COL_CC), col(COL_CX), col(COL_CG), col(COL_DX), col(COL_DG),
                  const((SCONV, BRANCH)), const((LRU_CONV, BRANCH)), const((1, BRANCH)),
                  const(wbd.shape), const((1, BRANCH)), const((1, BRANCH)), const((1, BRANCH))],
        out_specs=(out_spec, out_spec),
        scratch_shapes=[pltpu.VMEM((tt + SUBLANES, BRANCH), F32),
                        pltpu.VMEM((tt + SUBLANES, BRANCH), F32),
                        pltpu.VMEM((SUBLANES, BRANCH), F32)],
        compiler_params=_cparams(("parallel", "arbitrary")),
        name="sconv_rglru",
    )(proj, proj, proj, proj, proj, proj, p["sconv_w"], p["lru_conv_w"], vec(p["lru_conv_b"]), wbd,
      vec(p["lru_b_a"]), vec(p["lru_b_x"]), vec(p["lru_lambda"]))


def _merge_kernel(ya_ref, yb_ref, yc_ref, yd_ref, lg_ref, x_ref, gate_ref, bg_ref, wb_ref, wo_ref,
                  lnw_ref, lnb_ref, o_ref):
    m = None
    for k, y_ref in enumerate((ya_ref, yb_ref, yc_ref, yd_ref)):
        gk = jax.nn.sigmoid(lg_ref[:, k * D_MODEL:(k + 1) * D_MODEL].astype(F32) + bg_ref[k:k + 1, :])
        t = gk * jnp.dot(y_ref[...], wb_ref[k], preferred_element_type=F32)
        m = t if m is None else m + t
    out = jnp.dot(m.astype(BF16), wo_ref[...], preferred_element_type=F32)
    r = ALPHA * x_ref[...] + gate_ref[0] * out
    mu = jnp.mean(r, axis=-1, keepdims=True)
    rc = r - mu
    var = jnp.mean(jnp.square(rc), axis=-1, keepdims=True)
    o_ref[...] = rc * lax.rsqrt(var + LN_EPS) * lnw_ref[...] + lnb_ref[...]


def _merge(ya, yb, yc, yd, proj, x2, ada3, b_gate, w_branch, w_out, ln_w, ln_b, layer, seq, tm):
    m, d = x2.shape
    per_b = seq // tm
    ycol = pl.BlockSpec((tm, BRANCH), lambda i: (i, 0))
    once = pl.Buffered(1)
    return pl.pallas_call(
        _merge_kernel,
        out_shape=jax.ShapeDtypeStruct((m, d), F32),
        grid=(m // tm,),
        in_specs=[ycol, ycol, ycol, ycol,
                  pl.BlockSpec((tm, N_BRANCH * d), lambda i: (i, 0)),
                  pl.BlockSpec((tm, d), lambda i: (i, 0)),
                  pl.BlockSpec((1, 1, d), lambda i: (i // per_b, 0, 2)),
                  pl.BlockSpec((N_BRANCH, d), lambda i: (0, 0)),
                  pl.BlockSpec((None, N_BRANCH, BRANCH, d), lambda i: (layer, 0, 0, 0), pipeline_mode=once),
                  pl.BlockSpec((None, d, d), lambda i: (layer, 0, 0), pipeline_mode=once),
                  pl.BlockSpec((1, d), lambda i: (0, 0)),
                  pl.BlockSpec((1, d), lambda i: (0, 0))],
        out_specs=pl.BlockSpec((tm, d), lambda i: (i, 0)),
        compiler_params=_cparams(("parallel",)),
        name="merge",
    )(ya, yb, yc, yd, proj, x2, ada3, b_gate, w_branch, w_out, ln_w.reshape(1, -1), ln_b.reshape(1, -1))


def kernel(x, c, positions, w_ada, b_ada, w_in, b_gate, ssd_conv_w, ssd_conv_b, ssd_dt_bias, ssd_a_log,
           ssd_d, ssd_norm_w, attn_sinks, sconv_w, lru_conv_w, lru_conv_b, lru_w_a, lru_b_a, lru_w_x,
           lru_b_x, lru_lambda, w_branch, w_out, ln_w, ln_b):
    bsz, seq, d = x.shape
    small = dict(b_gate=b_gate, ssd_conv_w=ssd_conv_w, ssd_conv_b=ssd_conv_b, ssd_dt_bias=ssd_dt_bias,
                 ssd_a_log=ssd_a_log, ssd_d=ssd_d, ssd_norm_w=ssd_norm_w, attn_sinks=attn_sinks,
                 sconv_w=sconv_w, lru_conv_w=lru_conv_w, lru_conv_b=lru_conv_b, lru_w_a=lru_w_a,
                 lru_b_a=lru_b_a, lru_w_x=lru_w_x, lru_b_x=lru_b_x, lru_lambda=lru_lambda,
                 ln_w=ln_w, ln_b=ln_b)
    w_main, w_dt = _rearrange_w_in(w_in)
    w_branch_b = w_branch.astype(BF16)
    w_out_b = w_out.astype(BF16)
    cos_t, sin_t = _rope_tables(positions, tt=min(2048, bsz * seq))
    x2 = x.reshape(bsz * seq, d)
    for layer in range(DEPTH):
        p = {k: v[layer] for k, v in small.items()}
        ada3 = _ada(c, w_ada, b_ada, layer).reshape(bsz, 1, 3 * d)
        proj, dt_raw = _inproj(x2, ada3, w_main, w_dt, layer, seq, tm=min(1024, seq), tn=1536)
        ya, yb = _ssd_swa(proj, dt_raw, cos_t, sin_t, p, bsz, seq, tt=min(512, seq))
        yc, yd = _cd(proj, p, bsz, seq, tt=min(256, seq))
        x2 = _merge(ya, yb, yc, yd, proj, x2, ada3, p["b_gate"], w_branch_b, w_out_b, p["ln_w"], p["ln_b"],
                    layer, seq, tm=min(256, seq))
    return x2.reshape(bsz, seq, d)
```
